```python
import math
import jax
import jax.numpy as jnp
from jax import lax
import numpy as np

D_MODEL = 1024
BATCH = 8
SEQ = 4096
DEPTH = 4

N_MIXERS = 3
D_FF = 4 * D_MODEL
NORM_EPS = 1e-6
ROPE_THETA = 10000.0
OUT_SCALE = 0.5

RET_HEADS = 4
RET_DK = D_MODEL // RET_HEADS
RET_DV = 2 * RET_DK
RET_QK = RET_HEADS * RET_DK
RET_V = RET_HEADS * RET_DV
RET_CHUNK = 128

SWA_HEADS = 16
SWA_KV_HEADS = 4
SWA_HEAD_DIM = D_MODEL // SWA_HEADS
SWA_GROUP = SWA_HEADS // SWA_KV_HEADS
SWA_WINDOW = 128

GDN_QK_HEADS = 8
GDN_V_HEADS = 16
GDN_HEAD_DIM = 128
GDN_QK = GDN_QK_HEADS * GDN_HEAD_DIM
GDN_V = GDN_V_HEADS * GDN_HEAD_DIM
GDN_CONV_DIM = 2 * GDN_QK + GDN_V
GDN_CONV = 4
GDN_CHUNK = 64

N_RET = (DEPTH + N_MIXERS - 1) // N_MIXERS
N_SWA = (DEPTH + N_MIXERS - 2) // N_MIXERS
N_GDN = (DEPTH + N_MIXERS - 3) // N_MIXERS

kernel_name = "hybrid_retention_swa_gdn_trunk"


def rmsnorm(x, gain):
    xf = x.astype(jnp.float32)
    y = xf * lax.rsqrt(jnp.mean(xf * xf, axis=-1, keepdims=True) + NORM_EPS)
    return (y * gain.astype(jnp.float32)).astype(x.dtype)


def l2norm(x):
    xf = x.astype(jnp.float32)
    return xf * lax.rsqrt(jnp.sum(xf * xf, axis=-1, keepdims=True) + NORM_EPS)


def rope(x, positions):
    d = x.shape[-1]
    inv_freq = ROPE_THETA ** (-jnp.arange(0, d, 2, dtype=jnp.float32) / d)
    ang = positions.astype(jnp.float32)[:, :, None, None] * inv_freq
    cos, sin = jnp.cos(ang), jnp.sin(ang)
    xf = x.astype(jnp.float32)
    x1, x2 = xf[..., : d // 2], xf[..., d // 2:]
    return jnp.concatenate([x1 * cos - x2 * sin, x2 * cos + x1 * sin], axis=-1).astype(x.dtype)


def retention(h, positions, w_in, w_out):
    b, s, _ = h.shape
    c = RET_CHUNK
    n = s // c
    nh = RET_HEADS
    q, k, v, g = jnp.split(h @ w_in, [RET_QK, 2 * RET_QK, 2 * RET_QK + RET_V], axis=-1)
    q = rope(q.reshape(b, s, nh, RET_DK), positions)
    k = rope(k.reshape(b, s, nh, RET_DK), positions)

    def chunks(t):
        return t.astype(jnp.float32).reshape(b, n, c, nh, -1).transpose(0, 3, 1, 2, 4)

    qc = chunks(q)
    kc = chunks(k) * (RET_DK ** -0.5)
    vc = chunks(v.reshape(b, s, nh, RET_DV))
    log_gamma = jnp.log1p(-(2.0 ** (-5.0 - jnp.arange(nh, dtype=jnp.float32))))
    idx = jnp.arange(c, dtype=jnp.float32)
    diff = idx[:, None] - idx[None, :]
    causal = diff >= 0
    dmat = jnp.where(causal, jnp.exp(log_gamma[:, None, None] * jnp.where(causal, diff, 0.0)), 0.0)
    scores = jnp.einsum("bhncd,bhnsd->bhncs", qc, kc) * dmat[None, :, None]
    o_intra = jnp.einsum("bhncs,bhnse->bhnce", scores, vc)
    q_decay = jnp.exp(log_gamma[:, None] * (idx + 1.0))[None, :, :, None]
    k_decay = jnp.exp(log_gamma[:, None] * (c - 1.0 - idx))[None, :, :, None]
    chunk_decay = jnp.exp(log_gamma * c)[None, :, None, None]

    def step(state, inp):
        q_n, k_n, v_n = inp
        o = jnp.einsum("bhcd,bhde->bhce", q_n * q_decay, state)
        state = state * chunk_decay + jnp.einsum("bhcd,bhce->bhde", k_n * k_decay, v_n)
        return state, o

    state0 = jnp.zeros((b, nh, RET_DK, RET_DV), jnp.float32)
    lead = lambda t: jnp.moveaxis(t, 2, 0)
    _, o_cross = lax.scan(step, state0, (lead(qc), lead(kc), lead(vc)))
    o = o_intra + jnp.moveaxis(o_cross, 0, 2)
    o = o.transpose(0, 2, 3, 1, 4).reshape(b, s, nh, RET_DV)
    mu = jnp.mean(o, axis=-1, keepdims=True)
    var = jnp.mean(jnp.square(o - mu), axis=-1, keepdims=True)
    o = (o - mu) * lax.rsqrt(var + NORM_EPS)
    y = o.reshape(b, s, RET_V) * jax.nn.silu(g.astype(jnp.float32))
    return y.astype(h.dtype) @ w_out


def sliding_window_attention(h, positions, w_in, q_gain, k_gain, sinks, w_out):
    b, s, _ = h.shape
    w = SWA_WINDOW
    n = s // w
    dh = SWA_HEAD_DIM
    q, k, v = jnp.split(h @ w_in, [SWA_HEADS * dh, (SWA_HEADS + SWA_KV_HEADS) * dh], axis=-1)
    q = rope(rmsnorm(q.reshape(b, s, SWA_HEADS, dh), q_gain), positions)
    k = rope(rmsnorm(k.reshape(b, s, SWA_KV_HEADS, dh), k_gain), positions)
    qb = q.reshape(b, n, w, SWA_KV_HEADS, SWA_GROUP, dh)

    def with_prev(t):
        tb = t.reshape(b, n, w, SWA_KV_HEADS, dh)
        prev = jnp.pad(tb[:, :-1], ((0, 0), (1, 0), (0, 0), (0, 0), (0, 0)))
        return jnp.concatenate([prev, tb], axis=2)

    kb = with_prev(k)
    vb = with_prev(v.reshape(b, s, SWA_KV_HEADS, dh))
    scores = jnp.einsum("bnqkgd,bnskd->bnkgqs", qb, kb).astype(jnp.float32) * (dh ** -0.5)
    qpos = w + jnp.arange(w)
    kpos = jnp.arange(2 * w)
    rel = qpos[:, None] - kpos[None, :]
    band = (rel >= 0) & (rel < w)
    has_prev = (jnp.arange(n)[:, None, None] > 0) | (kpos[None, None, :] >= w)
    mask = (band[None] & has_prev)[None, :, None, None]
    scores = jnp.where(mask, scores, -jnp.inf)
    sink = jnp.broadcast_to(
        sinks.astype(jnp.float32).reshape(SWA_KV_HEADS, SWA_GROUP)[None, None, :, :, None, None],
        scores.shape[:-1] + (1,))
    probs = jax.nn.softmax(jnp.concatenate([scores, sink], axis=-1), axis=-1)[..., :-1]
    o = jnp.einsum("bnkgqs,bnskd->bnqkgd", probs.astype(vb.dtype), vb)
    return o.reshape(b, s, SWA_HEADS * dh) @ w_out


def causal_depthwise_conv(x, w):
    ch = x.shape[-1]
    return lax.conv_general_dilated(
        x, w[:, None, :].astype(x.dtype), window_strides=(1,), padding=[(GDN_CONV - 1, 0)],
        dimension_numbers=("NWC", "WIO", "NWC"), feature_group_count=ch)


def chunk_gated_delta_rule(q, k, v, g, beta):
    b, s, nh, dk = q.shape
    dv = v.shape[-1]
    c = GDN_CHUNK
    n = s // c

    def chunks(t):
        return t.astype(jnp.float32).reshape(b, n, c, nh, -1).transpose(0, 3, 1, 2, 4)

    q, k, v = chunks(q), chunks(k), chunks(v)
    g = chunks(g[..., None])[..., 0]
    beta = chunks(beta[..., None])[..., 0]
    g = jnp.cumsum(g, axis=-1)
    tril = jnp.tril(jnp.ones((c, c), dtype=bool))
    strict = jnp.tril(jnp.ones((c, c), dtype=bool), -1)
    decay = jnp.exp(jnp.where(tril, g[..., :, None] - g[..., None, :], -jnp.inf))
    k_beta = k * beta[..., None]
    lower = jnp.where(strict, jnp.einsum("bhncd,bhnsd->bhncs", k_beta, k) * decay, 0.0)
    eye = jnp.eye(c, dtype=jnp.float32)
    t_inv = lax.linalg.triangular_solve(lower + eye, jnp.broadcast_to(eye, lower.shape),
                                        left_side=True, lower=True, unit_diagonal=True)
    u = jnp.einsum("bhncs,bhnse->bhnce", t_inv, v * beta[..., None])
    wk = jnp.einsum("bhncs,bhnsd->bhncd", t_inv, k_beta * jnp.exp(g)[..., None])
    attn = jnp.einsum("bhncd,bhnsd->bhncs", q, k) * decay
    q_in = q * jnp.exp(g)[..., None]
    g_last = g[..., -1]
    k_out = k * jnp.exp(g_last[..., None] - g)[..., None]

    def step(state, inp):
        u_n, w_n, q_n, a_n, k_n, gl_n = inp
        v_new = u_n - jnp.einsum("bhcd,bhde->bhce", w_n, state)
        o = jnp.einsum("bhcd,bhde->bhce", q_n, state) + jnp.einsum("bhcs,bhse->bhce", a_n, v_new)
        state = state * jnp.exp(gl_n)[..., None, None] + jnp.einsum("bhcd,bhce->bhde", k_n, v_new)
        return state, o

    lead = lambda t: jnp.moveaxis(t, 2, 0)
    state0 = jnp.zeros((b, nh, dk, dv), jnp.float32)
    _, o = lax.scan(step, state0, (lead(u), lead(wk), lead(q_in), lead(attn), lead(k_out), lead(g_last)))
    return o.transpose(1, 0, 3, 2, 4).reshape(b, s, nh, dv)


def gated_deltanet(h, w_in, conv_w, a_log, dt_bias, out_gain, w_out):
    b, s, _ = h.shape
    dh = GDN_HEAD_DIM
    qkv, z, beta_logit, a = jnp.split(
        h @ w_in, [GDN_CONV_DIM, GDN_CONV_DIM + GDN_V, GDN_CONV_DIM + GDN_V + GDN_V_HEADS], axis=-1)
    qkv = jax.nn.silu(causal_depthwise_conv(qkv, conv_w))
    q, k, v = jnp.split(qkv, [GDN_QK, 2 * GDN_QK], axis=-1)
    rep = GDN_V_HEADS // GDN_QK_HEADS
    q = jnp.repeat(l2norm(q.reshape(b, s, GDN_QK_HEADS, dh)) * (dh ** -0.5), rep, axis=2)
    k = jnp.repeat(l2norm(k.reshape(b, s, GDN_QK_HEADS, dh)), rep, axis=2)
    v = v.reshape(b, s, GDN_V_HEADS, dh)
    beta = jax.nn.sigmoid(beta_logit.astype(jnp.float32))
    g = -jnp.exp(a_log.astype(jnp.float32)) * jax.nn.softplus(a.astype(jnp.float32) + dt_bias.astype(jnp.float32))
    o = chunk_gated_delta_rule(q, k, v, g, beta)
    o = rmsnorm(o, out_gain) * jax.nn.silu(z.reshape(b, s, GDN_V_HEADS, dh).astype(jnp.float32))
    return o.reshape(b, s, GDN_V).astype(h.dtype) @ w_out


def squared_relu_mlp(h, w_up, w_down):
    return jnp.square(jax.nn.relu(h @ w_up)) @ w_down


def setup_inputs(seed: int = 0) -> dict:
    key = jax.random.key(seed)
    ks = jax.random.split(key, 20)

    def normal(k, shape, scale):
        return jax.random.normal(k, shape, jnp.float32) * scale

    def gain(k, shape):
        return 1.0 + 0.1 * jax.random.normal(k, shape, jnp.float32)

    x = normal(ks[0], (BATCH, SEQ, D_MODEL), 1.0)
    offsets = jax.random.randint(ks[1], (BATCH, 1), 0, SEQ, dtype=jnp.int32)
    positions = (offsets + jnp.arange(SEQ, dtype=jnp.int32)[None, :]).astype(jnp.int32)
    norm_mix = gain(ks[2], (DEPTH, D_MODEL))
    norm_mlp = gain(ks[3], (DEPTH, D_MODEL))
    w_up = normal(ks[4], (DEPTH, D_MODEL, D_FF), D_MODEL ** -0.5)
    w_down = normal(ks[5], (DEPTH, D_FF, D_MODEL), OUT_SCALE * D_FF ** -0.5)
    ret_w_in = normal(ks[6], (N_RET, D_MODEL, 2 * RET_QK + 2 * RET_V), D_MODEL ** -0.5)
    ret_w_out = normal(ks[7], (N_RET, RET_V, D_MODEL), OUT_SCALE * RET_V ** -0.5)
    swa_w_in = normal(ks[8], (N_SWA, D_MODEL, (SWA_HEADS + 2 * SWA_KV_HEADS) * SWA_HEAD_DIM), D_MODEL ** -0.5)
    swa_q_gain = gain(ks[9], (N_SWA, SWA_HEAD_DIM))
    swa_k_gain = gain(ks[10], (N_SWA, SWA_HEAD_DIM))
    swa_sinks = normal(ks[11], (N_SWA, SWA_HEADS), 0.5)
    swa_w_out = normal(ks[12], (N_SWA, SWA_HEADS * SWA_HEAD_DIM, D_MODEL), OUT_SCALE * (SWA_HEADS * SWA_HEAD_DIM) ** -0.5)
    gdn_w_in = normal(ks[13], (N_GDN, D_MODEL, GDN_CONV_DIM + GDN_V + 2 * GDN_V_HEADS), D_MODEL ** -0.5)
    gdn_conv_w = normal(ks[14], (N_GDN, GDN_CONV, GDN_CONV_DIM), GDN_CONV ** -0.5)
    gdn_a_log = jnp.log(jax.random.uniform(ks[15], (N_GDN, GDN_V_HEADS), jnp.float32, minval=1.0, maxval=16.0))
    dt = jnp.exp(jax.random.uniform(ks[16], (N_GDN, GDN_V_HEADS), jnp.float32,
                                    minval=math.log(1e-3), maxval=math.log(1e-1)))
    gdn_dt_bias = dt + jnp.log(-jnp.expm1(-dt))
    gdn_out_gain = gain(ks[17], (N_GDN, GDN_HEAD_DIM))
    gdn_w_out = normal(ks[18], (N_GDN, GDN_V, D_MODEL), OUT_SCALE * GDN_V ** -0.5)
    return {
        "x": x, "positions": positions,
        "norm_mix": norm_mix, "norm_mlp": norm_mlp, "w_up": w_up, "w_down": w_down,
        "ret_w_in": ret_w_in, "ret_w_out": ret_w_out,
        "swa_w_in": swa_w_in, "swa_q_gain": swa_q_gain, "swa_k_gain": swa_k_gain,
        "swa_sinks": swa_sinks, "swa_w_out": swa_w_out,
        "gdn_w_in": gdn_w_in, "gdn_conv_w": gdn_conv_w, "gdn_a_log": gdn_a_log,
        "gdn_dt_bias": gdn_dt_bias, "gdn_out_gain": gdn_out_gain, "gdn_w_out": gdn_w_out,
    }


def reference(x, positions, norm_mix, norm_mlp, w_up, w_down, ret_w_in, ret_w_out,
              swa_w_in, swa_q_gain, swa_k_gain, swa_sinks, swa_w_out,
              gdn_w_in, gdn_conv_w, gdn_a_log, gdn_dt_bias, gdn_out_gain, gdn_w_out):
    for i in range(DEPTH):
        m, j = i % N_MIXERS, i // N_MIXERS
        h = rmsnorm(x, norm_mix[i])
        if m == 0:
            y = retention(h, positions, ret_w_in[j], ret_w_out[j])
        elif m == 1:
            y = sliding_window_attention(h, positions, swa_w_in[j], swa_q_gain[j], swa_k_gain[j],
                                         swa_sinks[j], swa_w_out[j])
        else:
            y = gated_deltanet(h, gdn_w_in[j], gdn_conv_w[j], gdn_a_log[j], gdn_dt_bias[j],
                               gdn_out_gain[j], gdn_w_out[j])
        x = x + y.astype(x.dtype)
        h = rmsnorm(x, norm_mlp[i])
        x = x + squared_relu_mlp(h, w_up[i], w_down[i]).astype(x.dtype)
    return x
```

```python
import functools
import math

import numpy as np
import jax
import jax.numpy as jnp
from jax import lax
from jax.experimental import pallas as pl
from jax.experimental.pallas import tpu as pltpu

F32 = jnp.float32
BF16 = jnp.bfloat16

D_MODEL = 1024
D_FF = 4 * D_MODEL
NORM_EPS = 1e-6
ROPE_THETA = 10000.0

RET_HEADS = 4
RET_DK = 256
RET_DV = 512
RET_QK = RET_HEADS * RET_DK
RET_V = RET_HEADS * RET_DV
RET_CHUNK = 128

SWA_HEADS = 16
SWA_KV_HEADS = 4
SWA_HEAD_DIM = 64
SWA_GROUP = SWA_HEADS // SWA_KV_HEADS
SWA_WINDOW = 128
SWA_Q = SWA_HEADS * SWA_HEAD_DIM
SWA_KV = SWA_KV_HEADS * SWA_HEAD_DIM

GDN_QK_HEADS = 8
GDN_V_HEADS = 16
GDN_HEAD_DIM = 128
GDN_QK = GDN_QK_HEADS * GDN_HEAD_DIM
GDN_V = GDN_V_HEADS * GDN_HEAD_DIM
GDN_CONV_DIM = 2 * GDN_QK + GDN_V
GDN_CONV = 4
GDN_CHUNK = 64
GDN_REP = GDN_V_HEADS // GDN_QK_HEADS

LANES = 128
VMEM_LIMIT = 56 * 1024 * 1024

NT_DIMS = (((1,), (1,)), ((), ()))
TN_DIMS = (((0,), (0,)), ((), ()))


def _params(*sem):
    return pltpu.CompilerParams(dimension_semantics=sem, vmem_limit_bytes=VMEM_LIMIT)


def _sigmoid(x):
    return 1.0 / (1.0 + jnp.exp(-x))


def _rms(x, gain_row):
    ms = jnp.mean(x * x, axis=-1, keepdims=True)
    return x * lax.rsqrt(ms + NORM_EPS) * gain_row


def _rope_table_kernel(pos_ref, inv_ref, sign_ref, cos_ref, sin_ref):
    ang = pos_ref[...] * inv_ref[...]
    cos_ref[...] = jnp.cos(ang)
    sin_ref[...] = jnp.sin(ang) * sign_ref[...]


def _rope_tables(pos_col, head_dim, width):
    m = pos_col.shape[0]
    half = head_dim // 2
    inv = ROPE_THETA ** (-np.arange(0, head_dim, 2, dtype=np.float32) / head_dim)
    reps = width // head_dim
    inv_row = np.tile(np.concatenate([inv, inv]), reps)[None, :].astype(np.float32)
    sign_row = np.tile(np.concatenate([-np.ones(half), np.ones(half)]), reps)[None, :].astype(np.float32)
    tm = min(m, 1024)
    return pl.pallas_call(
        _rope_table_kernel,
        grid=(m // tm,),
        in_specs=[pl.BlockSpec((tm, 1), lambda i: (i, 0)),
                  pl.BlockSpec((1, width), lambda i: (0, 0)),
                  pl.BlockSpec((1, width), lambda i: (0, 0))],
        out_specs=[pl.BlockSpec((tm, width), lambda i: (i, 0)),
                   pl.BlockSpec((tm, width), lambda i: (i, 0))],
        out_shape=[jax.ShapeDtypeStruct((m, width), F32)] * 2,
        compiler_params=_params("parallel"),
        name="rope_tables",
    )(pos_col, jnp.asarray(inv_row), jnp.asarray(sign_row))


def _inproj_kernel(x_ref, g_ref, w_ref, o_ref, h_ref):
    @pl.when(pl.program_id(1) == 0)
    def _():
        h_ref[...] = _rms(x_ref[...], g_ref[...]).astype(BF16)

    o_ref[...] = jnp.dot(h_ref[...], w_ref[...], preferred_element_type=F32).astype(o_ref.dtype)


def _inproj_aux_kernel(x_ref, g_ref, w_ref, wa_ref, o_ref, oa_ref, h_ref):
    @pl.when(pl.program_id(1) == 0)
    def _():
        h = _rms(x_ref[...], g_ref[...]).astype(BF16)
        h_ref[...] = h
        oa_ref[...] = jnp.dot(h, wa_ref[...], preferred_element_type=F32)

    o_ref[...] = jnp.dot(h_ref[...], w_ref[...], preferred_element_type=F32).astype(o_ref.dtype)


def _inproj(x2d, gain, w, tn, w_aux=None):
    m, d = x2d.shape
    n = w.shape[1]
    tm = min(m, 1024)
    grid = (m // tm, n // tn)
    x_spec = pl.BlockSpec((tm, d), lambda i, j: (i, 0))
    g_spec = pl.BlockSpec((1, d), lambda i, j: (0, 0))
    w_spec = pl.BlockSpec((d, tn), lambda i, j: (0, j))
    o_spec = pl.BlockSpec((tm, tn), lambda i, j: (i, j))
    scratch = [pltpu.VMEM((tm, d), BF16)]
    if w_aux is None:
        return pl.pallas_call(
            _inproj_kernel, grid=grid,
            in_specs=[x_spec, g_spec, w_spec], out_specs=o_spec,
            out_shape=jax.ShapeDtypeStruct((m, n), BF16),
            scratch_shapes=scratch,
            compiler_params=_params("parallel", "arbitrary"),
            name="inproj",
        )(x2d, gain, w)
    na = w_aux.shape[1]
    return pl.pallas_call(
        _inproj_aux_kernel, grid=grid,
        in_specs=[x_spec, g_spec, w_spec, pl.BlockSpec((d, na), lambda i, j: (0, 0))],
        out_specs=[o_spec, pl.BlockSpec((tm, na), lambda i, j: (i, 0))],
        out_shape=[jax.ShapeDtypeStruct((m, n), BF16), jax.ShapeDtypeStruct((m, na), F32)],
        scratch_shapes=scratch,
        compiler_params=_params("parallel", "arbitrary"),
        name="inproj_aux",
    )(x2d, gain, w, w_aux)


def _outproj_mlp_kernel(x_ref, y_ref, wo_ref, g_ref, wu_ref, wd_ref, o_ref, h_ref):
    @pl.when(pl.program_id(1) == 0)
    def _():
        xn = x_ref[...] + jnp.dot(y_ref[...], wo_ref[...], preferred_element_type=F32)
        o_ref[...] = xn
        h_ref[...] = _rms(xn, g_ref[...]).astype(BF16)

    a = jnp.dot(h_ref[...], wu_ref[...], preferred_element_type=F32)
    a = jnp.maximum(a, 0.0)
    a = (a * a).astype(BF16)
    o_ref[...] += jnp.dot(a, wd_ref[...], preferred_element_type=F32)


def _outproj_mlp(x2d, y, w_out, gain, w_up, w_down, tf=512):
    m, d = x2d.shape
    ky = y.shape[1]
    ff = w_up.shape[1]
    tm = min(m, 1024)
    return pl.pallas_call(
        _outproj_mlp_kernel,
        grid=(m // tm, ff // tf),
        in_specs=[pl.BlockSpec((tm, d), lambda i, j: (i, 0)),
                  pl.BlockSpec((tm, ky), lambda i, j: (i, 0)),
                  pl.BlockSpec((ky, d), lambda i, j: (0, 0)),
                  pl.BlockSpec((1, d), lambda i, j: (0, 0)),
                  pl.BlockSpec((d, tf), lambda i, j: (0, j)),
                  pl.BlockSpec((tf, d), lambda i, j: (j, 0))],
        out_specs=pl.BlockSpec((tm, d), lambda i, j: (i, 0)),
        out_shape=jax.ShapeDtypeStruct((m, d), F32),
        scratch_shapes=[pltpu.VMEM((tm, d), BF16)],
        compiler_params=_params("parallel", "arbitrary"),
        name="outproj_mlp",
    )(x2d, y, w_out, gain, w_up, w_down)


def _ret_constants():
    c = RET_CHUNK
    idx = np.arange(c, dtype=np.float64)
    log_gamma = np.log1p(-(2.0 ** (-5.0 - np.arange(RET_HEADS, dtype=np.float64))))
    diff = idx[:, None] - idx[None, :]
    scale = RET_DK ** -0.5
    dmat = np.where(diff >= 0, np.exp(log_gamma[:, None, None] * np.maximum(diff, 0.0)), 0.0) * scale
    qd = np.exp(log_gamma[:, None] * (idx + 1.0))
    kd = np.exp(log_gamma[:, None] * (c - 1.0 - idx)) * scale
    qd = np.broadcast_to(qd[:, :, None], (RET_HEADS, c, RET_DK))
    kd = np.broadcast_to(kd[:, :, None], (RET_HEADS, c, RET_DK))
    chunk_decay = [float(v) for v in np.exp(log_gamma * c)]
    return (jnp.asarray(dmat, F32), jnp.asarray(qd, F32), jnp.asarray(kd, F32), chunk_decay)


def _ret_kernel(chunk_decay, q_ref, k_ref, v_ref, g_ref, cos_ref, sin_ref, dmat_ref, qd_ref, kd_ref,
                o_ref, s_ref):
    @pl.when(pl.program_id(1) == 0)
    def _():
        s_ref[...] = jnp.zeros_like(s_ref)

    cos = cos_ref[...]
    sin = sin_ref[...]
    half = RET_DK // 2

    def rope(t):
        return t * cos + jnp.concatenate([t[:, half:], t[:, :half]], axis=1) * sin

    for h in range(RET_HEADS):
        qr = rope(q_ref[:, h * RET_DK:(h + 1) * RET_DK].astype(F32))
        kr = rope(k_ref[:, h * RET_DK:(h + 1) * RET_DK].astype(F32))
        v = v_ref[:, h * RET_DV:(h + 1) * RET_DV]
        scores = lax.dot_general(qr.astype(BF16), kr.astype(BF16), NT_DIMS,
                                 preferred_element_type=F32) * dmat_ref[h]
        o = jnp.dot(scores.astype(BF16), v, preferred_element_type=F32)
        state = s_ref[h]
        o = o + jnp.dot((qr * qd_ref[h]).astype(BF16), state.astype(BF16), preferred_element_type=F32)
        kdec = (kr * kd_ref[h]).astype(BF16)
        s_ref[h] = state * chunk_decay[h] + lax.dot_general(kdec, v, TN_DIMS, preferred_element_type=F32)
        mu = jnp.mean(o, axis=-1, keepdims=True)
        oc = o - mu
        var = jnp.mean(oc * oc, axis=-1, keepdims=True)
        g = g_ref[:, h * RET_DV:(h + 1) * RET_DV].astype(F32)
        y = oc * lax.rsqrt(var + NORM_EPS) * (g * _sigmoid(g))
        o_ref[:, h * RET_DV:(h + 1) * RET_DV] = y.astype(o_ref.dtype)


def _retention_core(qkvg, cos_t, sin_t, b, s):
    c = RET_CHUNK
    n = s // c
    m = b * s
    dmat, qd, kd, chunk_decay = _ret_constants()
    row = lambda bi, ni: bi * n + ni
    const3 = lambda bi, ni: (0, 0, 0)
    return pl.pallas_call(
        functools.partial(_ret_kernel, chunk_decay),
        grid=(b, n),
        in_specs=[pl.BlockSpec((c, RET_QK), lambda bi, ni: (row(bi, ni), 0)),
                  pl.BlockSpec((c, RET_QK), lambda bi, ni: (row(bi, ni), 1)),
                  pl.BlockSpec((c, RET_V), lambda bi, ni: (row(bi, ni), 1)),
                  pl.BlockSpec((c, RET_V), lambda bi, ni: (row(bi, ni), 2)),
                  pl.BlockSpec((c, RET_DK), lambda bi, ni: (row(bi, ni), 0)),
                  pl.BlockSpec((c, RET_DK), lambda bi, ni: (row(bi, ni), 0)),
                  pl.BlockSpec((RET_HEADS, c, c), const3),
                  pl.BlockSpec((RET_HEADS, c, RET_DK), const3),
                  pl.BlockSpec((RET_HEADS, c, RET_DK), const3)],
        out_specs=pl.BlockSpec((c, RET_V), lambda bi, ni: (row(bi, ni), 0)),
        out_shape=jax.ShapeDtypeStruct((m, RET_V), BF16),
        scratch_shapes=[pltpu.VMEM((RET_HEADS, RET_DK, RET_DV), F32)],
        compiler_params=_params("parallel", "arbitrary"),
        name="retention_core",
    )(qkvg, qkvg, qkvg, qkvg, cos_t, sin_t, dmat, qd, kd)


def _swa_kernel(sink_ref, q_ref, kv_ref, cos_ref, sin_ref, qg_ref, kg_ref, o_ref, kp_ref, vp_ref):
    w = SWA_WINDOW
    dh = SWA_HEAD_DIM
    ni = pl.program_id(1)

    @pl.when(ni == 0)
    def _():
        kp_ref[...] = jnp.zeros_like(kp_ref)
        vp_ref[...] = jnp.zeros_like(vp_ref)

    cos = cos_ref[...]
    sin = sin_ref[...]

    def rope(t):
        width = t.shape[1]
        reps = width // LANES
        c = jnp.concatenate([cos] * reps, axis=1)
        s = jnp.concatenate([sin] * reps, axis=1)
        lane = lax.broadcasted_iota(jnp.int32, t.shape, 1)
        first = (lane % dh) < (dh // 2)
        partner = jnp.where(first, pltpu.roll(t, width - dh // 2, 1), pltpu.roll(t, dh // 2, 1))
        return t * c + partner * s

    def inv_rms(sq, h):
        ss = jnp.sum(sq[:, h * dh:(h + 1) * dh], axis=-1, keepdims=True)
        return lax.rsqrt(ss * (1.0 / dh) + NORM_EPS)

    qf = q_ref[...].astype(F32)
    q_sq = qf * qf
    q_rot = rope(qf * qg_ref[...])
    kf = kv_ref[:, :SWA_KV].astype(F32)
    k_sq = kf * kf
    k_rot = rope(kf * kg_ref[...])
    v_cur = kv_ref[:, SWA_KV:]

    trow = lax.broadcasted_iota(jnp.int32, (w, 2 * w), 0)
    col = lax.broadcasted_iota(jnp.int32, (w, 2 * w), 1)
    rel = trow + w - col
    first_key = jnp.where(ni > 0, 0, w)
    valid = (rel >= 0) & (rel < w) & (col >= first_key)

    k_new = []
    outs = []
    for kh in range(SWA_KV_HEADS):
        k_h = (k_rot[:, kh * dh:(kh + 1) * dh] * inv_rms(k_sq, kh)).astype(BF16)
        k_new.append(k_h)
        k2 = jnp.concatenate([kp_ref[:, kh * dh:(kh + 1) * dh], k_h], axis=0)
        v2 = jnp.concatenate([vp_ref[:, kh * dh:(kh + 1) * dh], v_cur[:, kh * dh:(kh + 1) * dh]], axis=0)
        for gi in range(SWA_GROUP):
            h = kh * SWA_GROUP + gi
            q_h = q_rot[:, h * dh:(h + 1) * dh].astype(BF16)
            sc = lax.dot_general(q_h, k2, NT_DIMS, preferred_element_type=F32)
            sc = sc * (inv_rms(q_sq, h) * (dh ** -0.5))
            sc = jnp.where(valid, sc, -jnp.inf)
            sink = sink_ref[h]
            mx = jnp.maximum(jnp.max(sc, axis=-1, keepdims=True), sink)
            p = jnp.exp(sc - mx)
            den = jnp.sum(p, axis=-1, keepdims=True) + jnp.exp(sink - mx)
            o_h = jnp.dot(p.astype(BF16), v2, preferred_element_type=F32) / den
            outs.append(o_h)
    o_ref[...] = jnp.concatenate(outs, axis=1).astype(o_ref.dtype)
    kp_ref[...] = jnp.concatenate(k_new, axis=1)
    vp_ref[...] = v_cur


def _swa_core(qkv, cos_t, sin_t, q_gain, k_gain, sinks, b, s):
    w = SWA_WINDOW
    n = s // w
    m = b * s
    qg = jnp.tile(q_gain.astype(F32), SWA_HEADS)[None, :]
    kg = jnp.tile(k_gain.astype(F32), SWA_KV_HEADS)[None, :]
    row = lambda bi, ni: bi * n + ni
    return pl.pallas_call(
        _swa_kernel,
        grid=(b, n),
        in_specs=[pl.BlockSpec(memory_space=pltpu.SMEM),
                  pl.BlockSpec((w, SWA_Q), lambda bi, ni: (row(bi, ni), 0)),
                  pl.BlockSpec((w, 2 * SWA_KV), lambda bi, ni: (row(bi, ni), SWA_Q // (2 * SWA_KV))),
                  pl.BlockSpec((w, LANES), lambda bi, ni: (row(bi, ni), 0)),
                  pl.BlockSpec((w, LANES), lambda bi, ni: (row(bi, ni), 0)),
                  pl.BlockSpec((1, SWA_Q), lambda bi, ni: (0, 0)),
                  pl.BlockSpec((1, SWA_KV), lambda bi, ni: (0, 0))],
        out_specs=pl.BlockSpec((w, SWA_Q), lambda bi, ni: (row(bi, ni), 0)),
        out_shape=jax.ShapeDtypeStruct((m, SWA_Q), BF16),
        scratch_shapes=[pltpu.VMEM((w, SWA_KV), BF16), pltpu.VMEM((w, SWA_KV), BF16)],
        compiler_params=_params("parallel", "arbitrary"),
        name="swa_core",
    )(sinks.astype(F32), qkv, qkv, cos_t, sin_t, qg, kg)


GDN_HALO = 8


def _gdn_kernel(qkv_ref, z_ref, ba_ref, cw_ref, alog_ref, dtb_ref, og_ref, o_ref,
                ext_ref, q_scr, k_scr, v_scr, s_ref):
    c = GDN_CHUNK
    dh = GDN_HEAD_DIM
    nh = GDN_V_HEADS

    @pl.when(pl.program_id(1) == 0)
    def _():
        s_ref[...] = jnp.zeros_like(s_ref)
        ext_ref[0:GDN_HALO, :] = jnp.zeros((GDN_HALO, GDN_CONV_DIM), F32)

    ext_ref[GDN_HALO:GDN_HALO + c, :] = qkv_ref[...].astype(F32)
    for j in range(GDN_CONV_DIM // dh):
        cs = slice(j * dh, (j + 1) * dh)
        acc = ext_ref[GDN_HALO:GDN_HALO + c, cs] * cw_ref[GDN_CONV - 1:GDN_CONV, cs]
        for d in range(1, GDN_CONV):
            acc = acc + ext_ref[GDN_HALO - d:GDN_HALO - d + c, cs] * cw_ref[GDN_CONV - 1 - d:GDN_CONV - d, cs]
        act = acc * _sigmoid(acc)
        if j < 2 * GDN_QK_HEADS:
            nrm = act * lax.rsqrt(jnp.sum(act * act, axis=-1, keepdims=True) + NORM_EPS)
            if j < GDN_QK_HEADS:
                q_scr[:, cs] = nrm * (dh ** -0.5)
            else:
                k_scr[:, j * dh - GDN_QK:(j + 1) * dh - GDN_QK] = nrm
        else:
            v_scr[:, j * dh - 2 * GDN_QK:(j + 1) * dh - 2 * GDN_QK] = act
    ext_ref[0:GDN_HALO, :] = ext_ref[c:c + GDN_HALO, :]

    ba = ba_ref[...]
    beta = _sigmoid(ba)
    pre = ba + dtb_ref[...]
    softplus = jnp.maximum(pre, 0.0) + jnp.log1p(jnp.exp(-jnp.abs(pre)))
    g = -jnp.exp(alog_ref[...]) * softplus
    ri = lax.broadcasted_iota(jnp.int32, (c, c), 0)
    ci = lax.broadcasted_iota(jnp.int32, (c, c), 1)
    tril = ri >= ci
    strict = ri > ci
    gc = jnp.dot(tril.astype(F32), g, preferred_element_type=F32, precision=lax.Precision.HIGHEST)
    gc_t = gc.T
    eg = jnp.exp(gc)
    g_last = gc[c - 1:c, :]
    eg_out = jnp.exp(g_last - gc)
    eg_last = jnp.exp(g_last)
    eye = (ri == ci).astype(F32)

    for hq in range(GDN_QK_HEADS):
        q = q_scr[:, hq * dh:(hq + 1) * dh]
        k = k_scr[:, hq * dh:(hq + 1) * dh]
        kb = k.astype(BF16)
        kk = lax.dot_general(kb, kb, NT_DIMS, preferred_element_type=F32)
        qk = lax.dot_general(q.astype(BF16), kb, NT_DIMS, preferred_element_type=F32)
        for r in range(GDN_REP):
            hv = hq * GDN_REP + r
            gl = nh + hv
            b_col = beta[:, hv:hv + 1]
            g_col = gc[:, gl:gl + 1]
            g_row = gc_t[gl:gl + 1, :]
            eg_col = eg[:, gl:gl + 1]
            diff = g_col - g_row
            decay = jnp.where(tril, jnp.exp(jnp.where(tril, diff, 0.0)), 0.0)
            low = jnp.where(strict, kk * b_col * decay, 0.0)
            attn = qk * decay
            lb = low.astype(BF16)
            t_inv = eye - low
            pw = jnp.dot(lb, lb, preferred_element_type=F32)
            steps = int(math.log2(c)) - 1
            for it in range(steps):
                pwb = pw.astype(BF16)
                t_inv = t_inv + jnp.dot(t_inv.astype(BF16), pwb, preferred_element_type=F32)
                if it + 1 < steps:
                    pw = jnp.dot(pwb, pwb, preferred_element_type=F32)
            v = v_scr[:, hv * dh:(hv + 1) * dh]
            rhs = jnp.concatenate([v * b_col, k * (b_col * eg_col)], axis=1).astype(BF16)
            uw = jnp.dot(t_inv.astype(BF16), rhs, preferred_element_type=F32)
            state = s_ref[hv]
            sb = state.astype(BF16)
            v_new = uw[:, :dh] - jnp.dot(uw[:, dh:].astype(BF16), sb, preferred_element_type=F32)
            vnb = v_new.astype(BF16)
            o = (jnp.dot((q * eg_col).astype(BF16), sb, preferred_element_type=F32)
                 + jnp.dot(attn.astype(BF16), vnb, preferred_element_type=F32))
            k_out = (k * eg_out[:, gl:gl + 1]).astype(BF16)
            s_ref[hv] = state * eg_last[:, gl:gl + 1] + lax.dot_general(
                k_out, vnb, TN_DIMS, preferred_element_type=F32)
            zz = z_ref[:, hv * dh:(hv + 1) * dh].astype(F32)
            o_ref[:, hv * dh:(hv + 1) * dh] = (_rms(o, og_ref[...]) * (zz * _sigmoid(zz))).astype(o_ref.dtype)


def _gdn_core(qkvz, ba, conv_w, alog_row, dtb_row, out_gain, b, s):
    c = GDN_CHUNK
    n = s // c
    m = b * s
    row = lambda bi, ni: bi * n + ni
    const2 = lambda bi, ni: (0, 0)
    return pl.pallas_call(
        _gdn_kernel,
        grid=(b, n),
        in_specs=[pl.BlockSpec((c, GDN_CONV_DIM), lambda bi, ni: (row(bi, ni), 0)),
                  pl.BlockSpec((c, GDN_V), lambda bi, ni: (row(bi, ni), GDN_CONV_DIM // GDN_V)),
                  pl.BlockSpec((c, LANES), lambda bi, ni: (row(bi, ni), 0)),
                  pl.BlockSpec((GDN_CONV, GDN_CONV_DIM), const2),
                  pl.BlockSpec((1, LANES), const2),
                  pl.BlockSpec((1, LANES), const2),
                  pl.BlockSpec((1, GDN_HEAD_DIM), const2)],
        out_specs=pl.BlockSpec((c, GDN_V), lambda bi, ni: (row(bi, ni), 0)),
        out_shape=jax.ShapeDtypeStruct((m, GDN_V), BF16),
        scratch_shapes=[pltpu.VMEM((c + GDN_HALO, GDN_CONV_DIM), F32),
                        pltpu.VMEM((c, GDN_QK), F32),
                        pltpu.VMEM((c, GDN_QK), F32),
                        pltpu.VMEM((c, GDN_V), F32),
                        pltpu.VMEM((GDN_V_HEADS, GDN_HEAD_DIM, GDN_HEAD_DIM), F32)],
        compiler_params=_params("parallel", "arbitrary"),
        name="gdn_core",
    )(qkvz, qkvz, ba, conv_w, alog_row, dtb_row, out_gain)


def kernel(x, positions, norm_mix, norm_mlp, w_up, w_down, ret_w_in, ret_w_out, swa_w_in, swa_q_gain,
           swa_k_gain, swa_sinks, swa_w_out, gdn_w_in, gdn_conv_w, gdn_a_log, gdn_dt_bias, gdn_out_gain,
           gdn_w_out):
    b, s, d = x.shape
    m = b * s
    depth = norm_mix.shape[0]
    x2d = x.reshape(m, d)
    pos_col = positions.reshape(m, 1).astype(F32)
    ret_cos, ret_sin = _rope_tables(pos_col, RET_DK, RET_DK)
    swa_cos, swa_sin = _rope_tables(pos_col, SWA_HEAD_DIM, LANES)

    for i in range(depth):
        mixer, j = i % 3, i // 3
        gain = norm_mix[i][None, :]
        if mixer == 0:
            proj = _inproj(x2d, gain, ret_w_in[j].astype(BF16), tn=1024)
            y = _retention_core(proj, ret_cos, ret_sin, b, s)
            w_out = ret_w_out[j]
        elif mixer == 1:
            proj = _inproj(x2d, gain, swa_w_in[j].astype(BF16), tn=SWA_Q + 2 * SWA_KV)
            y = _swa_core(proj, swa_cos, swa_sin, swa_q_gain[j], swa_k_gain[j], swa_sinks[j], b, s)
            w_out = swa_w_out[j]
        else:
            n_main = GDN_CONV_DIM + GDN_V
            w_main = gdn_w_in[j][:, :n_main].astype(BF16)
            w_gate = jnp.pad(gdn_w_in[j][:, n_main:], ((0, 0), (0, LANES - 2 * GDN_V_HEADS))).astype(BF16)
            proj, ba = _inproj(x2d, gain, w_main, tn=1024, w_aux=w_gate)
            lane_pad = (GDN_V_HEADS, LANES - 2 * GDN_V_HEADS)
            alog_row = jnp.pad(gdn_a_log[j].astype(F32), lane_pad)[None, :]
            dtb_row = jnp.pad(gdn_dt_bias[j].astype(F32), lane_pad)[None, :]
            y = _gdn_core(proj, ba, gdn_conv_w[j].astype(F32), alog_row, dtb_row,
                          gdn_out_gain[j].astype(F32)[None, :], b, s)
            w_out = gdn_w_out[j]
        x2d = _outproj_mlp(x2d, y, w_out.astype(BF16), norm_mlp[i][None, :],
                           w_up[i].astype(BF16), w_down[i].astype(BF16))
    return x2d.reshape(b, s, d)
```

```python
import functools
import math

import numpy as np
import jax
import jax.numpy as jnp
from jax import lax
from jax.experimental import pallas as pl
from jax.experimental.pallas import tpu as pltpu

F32 = jnp.float32
BF16 = jnp.bfloat16

D_MODEL = 1024
D_FF = 4 * D_MODEL
NORM_EPS = 1e-6
ROPE_THETA = 10000.0

RET_HEADS = 4
RET_DK = 256
RET_DV = 512
RET_QK = RET_HEADS * RET_DK
RET_V = RET_HEADS * RET_DV
RET_CHUNK = 128

SWA_HEADS = 16
SWA_KV_HEADS = 4
SWA_HEAD_DIM = 64
SWA_GROUP = SWA_HEADS // SWA_KV_HEADS
SWA_WINDOW = 128
SWA_Q = SWA_HEADS * SWA_HEAD_DIM
SWA_KV = SWA_KV_HEADS * SWA_HEAD_DIM

GDN_QK_HEADS = 8
GDN_V_HEADS = 16
GDN_HEAD_DIM = 128
GDN_QK = GDN_QK_HEADS * GDN_HEAD_DIM
GDN_V = GDN_V_HEADS * GDN_HEAD_DIM
GDN_CONV_DIM = 2 * GDN_QK + GDN_V
GDN_CONV = 4
GDN_CHUNK = 64
GDN_REP = GDN_V_HEADS // GDN_QK_HEADS

LANES = 128
VMEM_LIMIT = 56 * 1024 * 1024

NT_DIMS = (((1,), (1,)), ((), ()))
TN_DIMS = (((0,), (0,)), ((), ()))


def _params(*sem):
    return pltpu.CompilerParams(dimension_semantics=sem, vmem_limit_bytes=VMEM_LIMIT)


def _sigmoid(x):
    return 1.0 / (1.0 + jnp.exp(-x))


def _rms(x, gain_row):
    ms = jnp.mean(x * x, axis=-1, keepdims=True)
    return x * lax.rsqrt(ms + NORM_EPS) * gain_row


def _rope_table_kernel(pos_ref, inv_ref, sign_ref, cos_ref, sin_ref):
    ang = pos_ref[...] * inv_ref[...]
    cos_ref[...] = jnp.cos(ang)
    sin_ref[...] = jnp.sin(ang) * sign_ref[...]


def _rope_tables(pos_col, head_dim, width):
    m = pos_col.shape[0]
    half = head_dim // 2
    inv = ROPE_THETA ** (-np.arange(0, head_dim, 2, dtype=np.float32) / head_dim)
    reps = width // head_dim
    inv_row = np.tile(np.concatenate([inv, inv]), reps)[None, :].astype(np.float32)
    sign_row = np.tile(np.concatenate([-np.ones(half), np.ones(half)]), reps)[None, :].astype(np.float32)
    tm = min(m, 1024)
    return pl.pallas_call(
        _rope_table_kernel,
        grid=(m // tm,),
        in_specs=[pl.BlockSpec((tm, 1), lambda i: (i, 0)),
                  pl.BlockSpec((1, width), lambda i: (0, 0)),
                  pl.BlockSpec((1, width), lambda i: (0, 0))],
        out_specs=[pl.BlockSpec((tm, width), lambda i: (i, 0)),
                   pl.BlockSpec((tm, width), lambda i: (i, 0))],
        out_shape=[jax.ShapeDtypeStruct((m, width), F32)] * 2,
        compiler_params=_params("parallel"),
        name="rope_tables",
    )(pos_col, jnp.asarray(inv_row), jnp.asarray(sign_row))


def _inproj_kernel(x_ref, g_ref, w_ref, o_ref, h_ref):
    @pl.when(pl.program_id(1) == 0)
    def _():
        h_ref[...] = _rms(x_ref[...], g_ref[...]).astype(BF16)

    o_ref[...] = jnp.dot(h_ref[...], w_ref[...], preferred_element_type=F32).astype(o_ref.dtype)


def _inproj_aux_kernel(x_ref, g_ref, w_ref, wa_ref, o_ref, oa_ref, h_ref):
    @pl.when(pl.program_id(1) == 0)
    def _():
        h = _rms(x_ref[...], g_ref[...]).astype(BF16)
        h_ref[...] = h
        oa_ref[...] = jnp.dot(h, wa_ref[...], preferred_element_type=F32)

    o_ref[...] = jnp.dot(h_ref[...], w_ref[...], preferred_element_type=F32).astype(o_ref.dtype)


def _inproj(x2d, gain, w, tn, w_aux=None):
    m, d = x2d.shape
    n = w.shape[1]
    tm = min(m, 1024)
    grid = (m // tm, n // tn)
    x_spec = pl.BlockSpec((tm, d), lambda i, j: (i, 0))
    g_spec = pl.BlockSpec((1, d), lambda i, j: (0, 0))
    w_spec = pl.BlockSpec((d, tn), lambda i, j: (0, j))
    o_spec = pl.BlockSpec((tm, tn), lambda i, j: (i, j))
    scratch = [pltpu.VMEM((tm, d), BF16)]
    if w_aux is None:
        return pl.pallas_call(
            _inproj_kernel, grid=grid,
            in_specs=[x_spec, g_spec, w_spec], out_specs=o_spec,
            out_shape=jax.ShapeDtypeStruct((m, n), BF16),
            scratch_shapes=scratch,
            compiler_params=_params("parallel", "arbitrary"),
            name="inproj",
        )(x2d, gain, w)
    na = w_aux.shape[1]
    return pl.pallas_call(
        _inproj_aux_kernel, grid=grid,
        in_specs=[x_spec, g_spec, w_spec, pl.BlockSpec((d, na), lambda i, j: (0, 0))],
        out_specs=[o_spec, pl.BlockSpec((tm, na), lambda i, j: (i, 0))],
        out_shape=[jax.ShapeDtypeStruct((m, n), BF16), jax.ShapeDtypeStruct((m, na), F32)],
        scratch_shapes=scratch,
        compiler_params=_params("parallel", "arbitrary"),
        name="inproj_aux",
    )(x2d, gain, w, w_aux)


def _outproj_mlp_kernel(x_ref, y_ref, wo_ref, g_ref, wu_ref, wd_ref, o_ref, h_ref):
    @pl.when(pl.program_id(1) == 0)
    def _():
        xn = x_ref[...] + jnp.dot(y_ref[...], wo_ref[...], preferred_element_type=F32)
        o_ref[...] = xn
        h_ref[...] = _rms(xn, g_ref[...]).astype(BF16)

    a = jnp.dot(h_ref[...], wu_ref[...], preferred_element_type=F32)
    a = jnp.maximum(a, 0.0)
    a = (a * a).astype(BF16)
    o_ref[...] += jnp.dot(a, wd_ref[...], preferred_element_type=F32)


def _outproj_mlp(x2d, y, w_out, gain, w_up, w_down, tf=512):
    m, d = x2d.shape
    ky = y.shape[1]
    ff = w_up.shape[1]
    tm = min(m, 1024)
    return pl.pallas_call(
        _outproj_mlp_kernel,
        grid=(m // tm, ff // tf),
        in_specs=[pl.BlockSpec((tm, d), lambda i, j: (i, 0)),
                  pl.BlockSpec((tm, ky), lambda i, j: (i, 0)),
                  pl.BlockSpec((ky, d), lambda i, j: (0, 0)),
                  pl.BlockSpec((1, d), lambda i, j: (0, 0)),
                  pl.BlockSpec((d, tf), lambda i, j: (0, j)),
                  pl.BlockSpec((tf, d), lambda i, j: (j, 0))],
        out_specs=pl.BlockSpec((tm, d), lambda i, j: (i, 0)),
        out_shape=jax.ShapeDtypeStruct((m, d), F32),
        scratch_shapes=[pltpu.VMEM((tm, d), BF16)],
        compiler_params=_params("parallel", "arbitrary"),
        name="outproj_mlp",
    )(x2d, y, w_out, gain, w_up, w_down)


def _ret_constants():
    c = RET_CHUNK
    idx = np.arange(c, dtype=np.float64)
    log_gamma = np.log1p(-(2.0 ** (-5.0 - np.arange(RET_HEADS, dtype=np.float64))))
    diff = idx[:, None] - idx[None, :]
    scale = RET_DK ** -0.5
    dmat = np.where(diff >= 0, np.exp(log_gamma[:, None, None] * np.maximum(diff, 0.0)), 0.0) * scale
    qd = np.exp(log_gamma[:, None] * (idx + 1.0))
    kd = np.exp(log_gamma[:, None] * (c - 1.0 - idx)) * scale
    qd = np.broadcast_to(qd[:, :, None], (RET_HEADS, c, RET_DK))
    kd = np.broadcast_to(kd[:, :, None], (RET_HEADS, c, RET_DK))
    chunk_decay = [float(v) for v in np.exp(log_gamma * c)]
    return (jnp.asarray(dmat, F32), jnp.asarray(qd, F32), jnp.asarray(kd, F32), chunk_decay)


def _ret_kernel(chunk_decay, q_ref, k_ref, v_ref, g_ref, cos_ref, sin_ref, dmat_ref, qd_ref, kd_ref,
                o_ref, s_ref):
    @pl.when(pl.program_id(1) == 0)
    def _():
        s_ref[...] = jnp.zeros_like(s_ref)

    cos = cos_ref[...]
    sin = sin_ref[...]
    half = RET_DK // 2

    def rope(t):
        return t * cos + jnp.concatenate([t[:, half:], t[:, :half]], axis=1) * sin

    heads = range(RET_HEADS)
    dot = functools.partial(jnp.dot, preferred_element_type=F32)
    qr = [rope(q_ref[:, h * RET_DK:(h + 1) * RET_DK].astype(F32)) for h in heads]
    kr = [rope(k_ref[:, h * RET_DK:(h + 1) * RET_DK].astype(F32)) for h in heads]
    v = [v_ref[:, h * RET_DV:(h + 1) * RET_DV] for h in heads]
    scores = [lax.dot_general(qr[h].astype(BF16), kr[h].astype(BF16), NT_DIMS, preferred_element_type=F32)
              * dmat_ref[h] for h in heads]
    state = [s_ref[h] for h in heads]
    o_cross = [dot((qr[h] * qd_ref[h]).astype(BF16), state[h].astype(BF16)) for h in heads]
    o = [o_cross[h] + dot(scores[h].astype(BF16), v[h]) for h in heads]
    for h in heads:
        kdec = (kr[h] * kd_ref[h]).astype(BF16)
        s_ref[h] = state[h] * chunk_decay[h] + lax.dot_general(kdec, v[h], TN_DIMS, preferred_element_type=F32)
    for h in heads:
        mu = jnp.mean(o[h], axis=-1, keepdims=True)
        oc = o[h] - mu
        var = jnp.mean(oc * oc, axis=-1, keepdims=True)
        g = g_ref[:, h * RET_DV:(h + 1) * RET_DV].astype(F32)
        y = oc * lax.rsqrt(var + NORM_EPS) * (g * _sigmoid(g))
        o_ref[:, h * RET_DV:(h + 1) * RET_DV] = y.astype(o_ref.dtype)


def _retention_core(qkvg, cos_t, sin_t, b, s):
    c = RET_CHUNK
    n = s // c
    m = b * s
    dmat, qd, kd, chunk_decay = _ret_constants()
    row = lambda bi, ni: bi * n + ni
    const3 = lambda bi, ni: (0, 0, 0)
    return pl.pallas_call(
        functools.partial(_ret_kernel, chunk_decay),
        grid=(b, n),
        in_specs=[pl.BlockSpec((c, RET_QK), lambda bi, ni: (row(bi, ni), 0)),
                  pl.BlockSpec((c, RET_QK), lambda bi, ni: (row(bi, ni), 1)),
                  pl.BlockSpec((c, RET_V), lambda bi, ni: (row(bi, ni), 1)),
                  pl.BlockSpec((c, RET_V), lambda bi, ni: (row(bi, ni), 2)),
                  pl.BlockSpec((c, RET_DK), lambda bi, ni: (row(bi, ni), 0)),
                  pl.BlockSpec((c, RET_DK), lambda bi, ni: (row(bi, ni), 0)),
                  pl.BlockSpec((RET_HEADS, c, c), const3),
                  pl.BlockSpec((RET_HEADS, c, RET_DK), const3),
                  pl.BlockSpec((RET_HEADS, c, RET_DK), const3)],
        out_specs=pl.BlockSpec((c, RET_V), lambda bi, ni: (row(bi, ni), 0)),
        out_shape=jax.ShapeDtypeStruct((m, RET_V), BF16),
        scratch_shapes=[pltpu.VMEM((RET_HEADS, RET_DK, RET_DV), F32)],
        compiler_params=_params("parallel", "arbitrary"),
        name="retention_core",
    )(qkvg, qkvg, qkvg, qkvg, cos_t, sin_t, dmat, qd, kd)


def _swa_kernel(sink_ref, q_ref, kv_ref, cos_ref, sin_ref, qg_ref, kg_ref, o_ref, kp_ref, vp_ref):
    w = SWA_WINDOW
    dh = SWA_HEAD_DIM
    ni = pl.program_id(1)

    @pl.when(ni == 0)
    def _():
        kp_ref[...] = jnp.zeros_like(kp_ref)
        vp_ref[...] = jnp.zeros_like(vp_ref)

    cos = cos_ref[...]
    sin = sin_ref[...]

    def rope(t):
        width = t.shape[1]
        reps = width // LANES
        c = jnp.concatenate([cos] * reps, axis=1)
        s = jnp.concatenate([sin] * reps, axis=1)
        lane = lax.broadcasted_iota(jnp.int32, t.shape, 1)
        first = (lane % dh) < (dh // 2)
        partner = jnp.where(first, pltpu.roll(t, width - dh // 2, 1), pltpu.roll(t, dh // 2, 1))
        return t * c + partner * s

    def inv_rms(sq, h):
        ss = jnp.sum(sq[:, h * dh:(h + 1) * dh], axis=-1, keepdims=True)
        return lax.rsqrt(ss * (1.0 / dh) + NORM_EPS)

    qf = q_ref[...].astype(F32)
    q_sq = qf * qf
    q_rot = rope(qf * qg_ref[...])
    kf = kv_ref[:, :SWA_KV].astype(F32)
    k_sq = kf * kf
    k_rot = rope(kf * kg_ref[...])
    v_cur = kv_ref[:, SWA_KV:]

    rows = SWA_GROUP * w
    trow = lax.broadcasted_iota(jnp.int32, (rows, 2 * w), 0) % w
    col = lax.broadcasted_iota(jnp.int32, (rows, 2 * w), 1)
    rel = trow + w - col
    first_key = jnp.where(ni > 0, 0, w)
    valid = (rel >= 0) & (rel < w) & (col >= first_key)

    kv_heads = range(SWA_KV_HEADS)
    k_new = [(k_rot[:, kh * dh:(kh + 1) * dh] * inv_rms(k_sq, kh)).astype(BF16) for kh in kv_heads]
    k2 = [jnp.concatenate([kp_ref[:, kh * dh:(kh + 1) * dh], k_new[kh]], axis=0) for kh in kv_heads]
    v2 = [jnp.concatenate([vp_ref[:, kh * dh:(kh + 1) * dh], v_cur[:, kh * dh:(kh + 1) * dh]], axis=0)
          for kh in kv_heads]
    q_st, scale, sink = [], [], []
    for kh in kv_heads:
        hs = [kh * SWA_GROUP + gi for gi in range(SWA_GROUP)]
        q_st.append(jnp.concatenate([q_rot[:, h * dh:(h + 1) * dh] for h in hs], axis=0).astype(BF16))
        scale.append(jnp.concatenate([inv_rms(q_sq, h) for h in hs], axis=0) * (dh ** -0.5))
        sink.append(jnp.concatenate([jnp.full((w, 1), sink_ref[h], F32) for h in hs], axis=0))
    sc = [lax.dot_general(q_st[kh], k2[kh], NT_DIMS, preferred_element_type=F32) for kh in kv_heads]
    sc = [jnp.where(valid, sc[kh] * scale[kh], -jnp.inf) for kh in kv_heads]
    mx = [jnp.maximum(jnp.max(sc[kh], axis=-1, keepdims=True), sink[kh]) for kh in kv_heads]
    p = [jnp.exp(sc[kh] - mx[kh]) for kh in kv_heads]
    den = [jnp.sum(p[kh], axis=-1, keepdims=True) + jnp.exp(sink[kh] - mx[kh]) for kh in kv_heads]
    o_st = [jnp.dot(p[kh].astype(BF16), v2[kh], preferred_element_type=F32) / den[kh] for kh in kv_heads]
    outs = [o_st[kh][gi * w:(gi + 1) * w, :] for kh in kv_heads for gi in range(SWA_GROUP)]
    o_ref[...] = jnp.concatenate(outs, axis=1).astype(o_ref.dtype)
    kp_ref[...] = jnp.concatenate(k_new, axis=1)
    vp_ref[...] = v_cur


def _swa_core(qkv, cos_t, sin_t, q_gain, k_gain, sinks, b, s):
    w = SWA_WINDOW
    n = s // w
    m = b * s
    qg = jnp.tile(q_gain.astype(F32), SWA_HEADS)[None, :]
    kg = jnp.tile(k_gain.astype(F32), SWA_KV_HEADS)[None, :]
    row = lambda bi, ni: bi * n + ni
    return pl.pallas_call(
        _swa_kernel,
        grid=(b, n),
        in_specs=[pl.BlockSpec(memory_space=pltpu.SMEM),
                  pl.BlockSpec((w, SWA_Q), lambda bi, ni: (row(bi, ni), 0)),
                  pl.BlockSpec((w, 2 * SWA_KV), lambda bi, ni: (row(bi, ni), SWA_Q // (2 * SWA_KV))),
                  pl.BlockSpec((w, LANES), lambda bi, ni: (row(bi, ni), 0)),
                  pl.BlockSpec((w, LANES), lambda bi, ni: (row(bi, ni), 0)),
                  pl.BlockSpec((1, SWA_Q), lambda bi, ni: (0, 0)),
                  pl.BlockSpec((1, SWA_KV), lambda bi, ni: (0, 0))],
        out_specs=pl.BlockSpec((w, SWA_Q), lambda bi, ni: (row(bi, ni), 0)),
        out_shape=jax.ShapeDtypeStruct((m, SWA_Q), BF16),
        scratch_shapes=[pltpu.VMEM((w, SWA_KV), BF16), pltpu.VMEM((w, SWA_KV), BF16)],
        compiler_params=_params("parallel", "arbitrary"),
        name="swa_core",
    )(sinks.astype(F32), qkv, qkv, cos_t, sin_t, qg, kg)


GDN_HALO = 8
GDN_STACK = 4
GDN_ROWS = GDN_STACK * GDN_CHUNK


def _gdn_masks():
    r = np.arange(GDN_ROWS)
    same = (r[:, None] // GDN_CHUNK) == (r[None, :] // GDN_CHUNK)
    tril = same & (r[:, None] >= r[None, :])
    strict = same & (r[:, None] > r[None, :])
    return jnp.asarray(tril, F32), jnp.asarray(strict, F32)


def _gdn_kernel(qkv_ref, z_ref, ba_ref, cw_ref, alog_ref, dtb_ref, og_ref, mt_ref, ms_ref, o_ref,
                ext_ref, q_scr, k_scr, v_scr, s_ref):
    c = GDN_CHUNK
    dh = GDN_HEAD_DIM
    nh = GDN_V_HEADS

    @pl.when(pl.program_id(1) == 0)
    def _():
        s_ref[...] = jnp.zeros_like(s_ref)
        ext_ref[0:GDN_HALO, :] = jnp.zeros((GDN_HALO, GDN_CONV_DIM), F32)

    ext_ref[GDN_HALO:GDN_HALO + c, :] = qkv_ref[...].astype(F32)
    for j in range(GDN_CONV_DIM // dh):
        cs = slice(j * dh, (j + 1) * dh)
        acc = ext_ref[GDN_HALO:GDN_HALO + c, cs] * cw_ref[GDN_CONV - 1:GDN_CONV, cs]
        for d in range(1, GDN_CONV):
            acc = acc + ext_ref[GDN_HALO - d:GDN_HALO - d + c, cs] * cw_ref[GDN_CONV - 1 - d:GDN_CONV - d, cs]
        act = acc * _sigmoid(acc)
        if j < 2 * GDN_QK_HEADS:
            nrm = act * lax.rsqrt(jnp.sum(act * act, axis=-1, keepdims=True) + NORM_EPS)
            if j < GDN_QK_HEADS:
                q_scr[:, cs] = nrm * (dh ** -0.5)
            else:
                k_scr[:, j * dh - GDN_QK:(j + 1) * dh - GDN_QK] = nrm
        else:
            v_scr[:, j * dh - 2 * GDN_QK:(j + 1) * dh - 2 * GDN_QK] = act
    ext_ref[0:GDN_HALO, :] = ext_ref[c:c + GDN_HALO, :]

    ba = ba_ref[...]
    beta = pltpu.roll(_sigmoid(ba), nh, 1)
    pre = ba + dtb_ref[...]
    softplus = jnp.maximum(pre, 0.0) + jnp.log1p(jnp.exp(-jnp.abs(pre)))
    g = -jnp.exp(alog_ref[...]) * softplus
    ri = lax.broadcasted_iota(jnp.int32, (c, c), 0)
    ci = lax.broadcasted_iota(jnp.int32, (c, c), 1)
    gc = jnp.dot((ri >= ci).astype(F32), g, preferred_element_type=F32, precision=lax.Precision.HIGHEST)
    gc_t = gc.T
    m_tril = mt_ref[...]
    m_strict = ms_ref[...]
    eye = m_tril - m_strict

    def stack_cols(a, lanes):
        return jnp.concatenate([jnp.broadcast_to(a[:, l:l + 1], (c, dh)) for l in lanes], axis=0)

    groups = range(nh // GDN_STACK)
    heads = [[grp * GDN_STACK + r for r in range(GDN_STACK)] for grp in groups]
    dot = functools.partial(jnp.dot, preferred_element_type=F32)

    g_col, b_col, g_last, eg, k_st, q_st, low, attn = [], [], [], [], [], [], [], []
    for grp in groups:
        lanes = [nh + hv for hv in heads[grp]]
        gcol = stack_cols(gc, lanes)
        g_row = jnp.concatenate([gc_t[l:l + 1, :] for l in lanes], axis=1)
        g_col.append(gcol)
        b_col.append(stack_cols(beta, lanes))
        g_last.append(jnp.concatenate(
            [jnp.broadcast_to(gcol[r * c + c - 1:r * c + c, :], (c, dh)) for r in range(GDN_STACK)], axis=0))
        eg.append(jnp.exp(gcol))
        decay = jnp.exp((jnp.concatenate([gcol, gcol], axis=1) - g_row) * m_tril) * m_tril
        k_st.append(jnp.concatenate(
            [k_scr[:, (hv // GDN_REP) * dh:(hv // GDN_REP + 1) * dh] for hv in heads[grp]], axis=0))
        q_st.append(jnp.concatenate(
            [q_scr[:, (hv // GDN_REP) * dh:(hv // GDN_REP + 1) * dh] for hv in heads[grp]], axis=0))
        kb = k_st[grp].astype(BF16)
        kk = lax.dot_general(kb, kb, NT_DIMS, preferred_element_type=F32)
        qk = lax.dot_general(q_st[grp].astype(BF16), kb, NT_DIMS, preferred_element_type=F32)
        low.append(kk * decay * jnp.concatenate([b_col[grp], b_col[grp]], axis=1) * m_strict)
        attn.append((qk * decay).astype(BF16))

    t_inv = [eye - low[grp] for grp in groups]
    pwb = [low[grp].astype(BF16) for grp in groups]
    for _ in range(int(math.log2(c)) - 1):
        pwb = [dot(pwb[grp], pwb[grp]).astype(BF16) for grp in groups]
        t_inv = [t_inv[grp] + dot(t_inv[grp].astype(BF16), pwb[grp]) for grp in groups]

    uw = []
    for grp in groups:
        v_st = jnp.concatenate([v_scr[:, hv * dh:(hv + 1) * dh] for hv in heads[grp]], axis=0)
        rhs = jnp.concatenate([v_st * b_col[grp], k_st[grp] * (b_col[grp] * eg[grp])], axis=1).astype(BF16)
        uw.append(dot(t_inv[grp].astype(BF16), rhs))

    states = [[s_ref[hv] for hv in heads[grp]] for grp in groups]
    sbs = [[st.astype(BF16) for st in states[grp]] for grp in groups]
    vnb = []
    for grp in groups:
        v_new = jnp.concatenate(
            [uw[grp][r * c:(r + 1) * c, :dh] - dot(uw[grp][r * c:(r + 1) * c, dh:].astype(BF16), sbs[grp][r])
             for r in range(GDN_STACK)], axis=0)
        vnb.append(v_new.astype(BF16))
    o_intra = [dot(attn[grp], vnb[grp]) for grp in groups]
    for grp in groups:
        q_in = (q_st[grp] * eg[grp]).astype(BF16)
        k_out = (k_st[grp] * jnp.exp(g_last[grp] - g_col[grp])).astype(BF16)
        for r, hv in enumerate(heads[grp]):
            rows = slice(r * c, (r + 1) * c)
            o = o_intra[grp][rows] + dot(q_in[rows], sbs[grp][r])
            s_ref[hv] = states[grp][r] * jnp.exp(g_last[grp][r * c:r * c + 1, :]) + lax.dot_general(
                k_out[rows], vnb[grp][rows], TN_DIMS, preferred_element_type=F32)
            zz = z_ref[:, hv * dh:(hv + 1) * dh].astype(F32)
            o_ref[:, hv * dh:(hv + 1) * dh] = (_rms(o, og_ref[...]) * (zz * _sigmoid(zz))).astype(o_ref.dtype)


def _gdn_core(qkvz, ba, conv_w, alog_row, dtb_row, out_gain, b, s):
    c = GDN_CHUNK
    n = s // c
    m = b * s
    m_tril, m_strict = _gdn_masks()
    row = lambda bi, ni: bi * n + ni
    const2 = lambda bi, ni: (0, 0)
    return pl.pallas_call(
        _gdn_kernel,
        grid=(b, n),
        in_specs=[pl.BlockSpec((c, GDN_CONV_DIM), lambda bi, ni: (row(bi, ni), 0)),
                  pl.BlockSpec((c, GDN_V), lambda bi, ni: (row(bi, ni), GDN_CONV_DIM // GDN_V)),
                  pl.BlockSpec((c, LANES), lambda bi, ni: (row(bi, ni), 0)),
                  pl.BlockSpec((GDN_CONV, GDN_CONV_DIM), const2),
                  pl.BlockSpec((1, LANES), const2),
                  pl.BlockSpec((1, LANES), const2),
                  pl.BlockSpec((1, GDN_HEAD_DIM), const2),
                  pl.BlockSpec((GDN_ROWS, GDN_ROWS), const2),
                  pl.BlockSpec((GDN_ROWS, GDN_ROWS), const2)],
        out_specs=pl.BlockSpec((c, GDN_V), lambda bi, ni: (row(bi, ni), 0)),
        out_shape=jax.ShapeDtypeStruct((m, GDN_V), BF16),
        scratch_shapes=[pltpu.VMEM((c + GDN_HALO, GDN_CONV_DIM), F32),
                        pltpu.VMEM((c, GDN_QK), F32),
                        pltpu.VMEM((c, GDN_QK), F32),
                        pltpu.VMEM((c, GDN_V), F32),
                        pltpu.VMEM((GDN_V_HEADS, GDN_HEAD_DIM, GDN_HEAD_DIM), F32)],
        compiler_params=_params("parallel", "arbitrary"),
        name="gdn_core",
    )(qkvz, qkvz, ba, conv_w, alog_row, dtb_row, out_gain, m_tril, m_strict)


def kernel(x, positions, norm_mix, norm_mlp, w_up, w_down, ret_w_in, ret_w_out, swa_w_in, swa_q_gain,
           swa_k_gain, swa_sinks, swa_w_out, gdn_w_in, gdn_conv_w, gdn_a_log, gdn_dt_bias, gdn_out_gain,
           gdn_w_out):
    b, s, d = x.shape
    m = b * s
    depth = norm_mix.shape[0]
    x2d = x.reshape(m, d)
    pos_col = positions.reshape(m, 1).astype(F32)
    ret_cos, ret_sin = _rope_tables(pos_col, RET_DK, RET_DK)
    swa_cos, swa_sin = _rope_tables(pos_col, SWA_HEAD_DIM, LANES)

    for i in range(depth):
        mixer, j = i % 3, i // 3
        gain = norm_mix[i][None, :]
        if mixer == 0:
            proj = _inproj(x2d, gain, ret_w_in[j].astype(BF16), tn=1024)
            y = _retention_core(proj, ret_cos, ret_sin, b, s)
            w_out = ret_w_out[j]
        elif mixer == 1:
            proj = _inproj(x2d, gain, swa_w_in[j].astype(BF16), tn=SWA_Q + 2 * SWA_KV)
            y = _swa_core(proj, swa_cos, swa_sin, swa_q_gain[j], swa_k_gain[j], swa_sinks[j], b, s)
            w_out = swa_w_out[j]
        else:
            n_main = GDN_CONV_DIM + GDN_V
            w_main = gdn_w_in[j][:, :n_main].astype(BF16)
            w_gate = jnp.pad(gdn_w_in[j][:, n_main:], ((0, 0), (0, LANES - 2 * GDN_V_HEADS))).astype(BF16)
            proj, ba = _inproj(x2d, gain, w_main, tn=1024, w_aux=w_gate)
            lane_pad = (GDN_V_HEADS, LANES - 2 * GDN_V_HEADS)
            alog_row = jnp.pad(gdn_a_log[j].astype(F32), lane_pad)[None, :]
            dtb_row = jnp.pad(gdn_dt_bias[j].astype(F32), lane_pad)[None, :]
            y = _gdn_core(proj, ba, gdn_conv_w[j].astype(F32), alog_row, dtb_row,
                          gdn_out_gain[j].astype(F32)[None, :], b, s)
            w_out = gdn_w_out[j]
        x2d = _outproj_mlp(x2d, y, w_out.astype(BF16), norm_mlp[i][None, :],
                           w_up[i].astype(BF16), w_down[i].astype(BF16))
    return x2d.reshape(b, s, d)
```

```python
import functools
import math

import numpy as np
import jax
import jax.numpy as jnp
from jax import lax
from jax.experimental import pallas as pl
from jax.experimental.pallas import tpu as pltpu

F32 = jnp.float32
BF16 = jnp.bfloat16

D_MODEL = 1024
D_FF = 4 * D_MODEL
NORM_EPS = 1e-6
ROPE_THETA = 10000.0

RET_HEADS = 4
RET_DK = 256
RET_DV = 512
RET_QK = RET_HEADS * RET_DK
RET_V = RET_HEADS * RET_DV
RET_CHUNK = 128

SWA_HEADS = 16
SWA_KV_HEADS = 4
SWA_HEAD_DIM = 64
SWA_GROUP = SWA_HEADS // SWA_KV_HEADS
SWA_WINDOW = 128
SWA_Q = SWA_HEADS * SWA_HEAD_DIM
SWA_KV = SWA_KV_HEADS * SWA_HEAD_DIM

GDN_QK_HEADS = 8
GDN_V_HEADS = 16
GDN_HEAD_DIM = 128
GDN_QK = GDN_QK_HEADS * GDN_HEAD_DIM
GDN_V = GDN_V_HEADS * GDN_HEAD_DIM
GDN_CONV_DIM = 2 * GDN_QK + GDN_V
GDN_CONV = 4
GDN_CHUNK = 64
GDN_REP = GDN_V_HEADS // GDN_QK_HEADS

LANES = 128
VMEM_LIMIT = 56 * 1024 * 1024
PROJ_ROWS = 512

NT_DIMS = (((1,), (1,)), ((), ()))
TN_DIMS = (((0,), (0,)), ((), ()))


def _params(*sem):
    return pltpu.CompilerParams(dimension_semantics=sem, vmem_limit_bytes=VMEM_LIMIT)


def _sigmoid(x):
    return 1.0 / (1.0 + jnp.exp(-x))


def _rms(x, gain_row):
    ms = jnp.mean(x * x, axis=-1, keepdims=True)
    return x * lax.rsqrt(ms + NORM_EPS) * gain_row


def _rope_table_kernel(pos_ref, inv_ref, sign_ref, cos_ref, sin_ref):
    ang = pos_ref[...] * inv_ref[...]
    cos_ref[...] = jnp.cos(ang)
    sin_ref[...] = jnp.sin(ang) * sign_ref[...]


def _rope_tables(pos_col, head_dim, width):
    m = pos_col.shape[0]
    half = head_dim // 2
    inv = ROPE_THETA ** (-np.arange(0, head_dim, 2, dtype=np.float32) / head_dim)
    reps = width // head_dim
    inv_row = np.tile(np.concatenate([inv, inv]), reps)[None, :].astype(np.float32)
    sign_row = np.tile(np.concatenate([-np.ones(half), np.ones(half)]), reps)[None, :].astype(np.float32)
    tm = min(m, 1024)
    return pl.pallas_call(
        _rope_table_kernel,
        grid=(m // tm,),
        in_specs=[pl.BlockSpec((tm, 1), lambda i: (i, 0)),
                  pl.BlockSpec((1, width), lambda i: (0, 0)),
                  pl.BlockSpec((1, width), lambda i: (0, 0))],
        out_specs=[pl.BlockSpec((tm, width), lambda i: (i, 0)),
                   pl.BlockSpec((tm, width), lambda i: (i, 0))],
        out_shape=[jax.ShapeDtypeStruct((m, width), F32)] * 2,
        compiler_params=_params("parallel"),
        name="rope_tables",
    )(pos_col, jnp.asarray(inv_row), jnp.asarray(sign_row))


def _resident(shape):
    return pl.BlockSpec(shape, lambda i: (0,) * len(shape), pipeline_mode=pl.Buffered(1))


def _inproj_kernel(tn, x_ref, g_ref, w_ref, o_ref):
    h = _rms(x_ref[...], g_ref[...]).astype(BF16)
    for j in range(o_ref.shape[1] // tn):
        cols = slice(j * tn, (j + 1) * tn)
        o_ref[:, cols] = jnp.dot(h, w_ref[:, cols], preferred_element_type=F32).astype(o_ref.dtype)


def _inproj_aux_kernel(tn, x_ref, g_ref, w_ref, wa_ref, o_ref, oa_ref):
    h = _rms(x_ref[...], g_ref[...]).astype(BF16)
    oa_ref[...] = jnp.dot(h, wa_ref[...], preferred_element_type=F32)
    for j in range(o_ref.shape[1] // tn):
        cols = slice(j * tn, (j + 1) * tn)
        o_ref[:, cols] = jnp.dot(h, w_ref[:, cols], preferred_element_type=F32).astype(o_ref.dtype)


def _inproj(x2d, gain, w, tn, w_aux=None):
    m, d = x2d.shape
    n = w.shape[1]
    tm = min(m, PROJ_ROWS)
    x_spec = pl.BlockSpec((tm, d), lambda i: (i, 0))
    o_spec = pl.BlockSpec((tm, n), lambda i: (i, 0))
    if w_aux is None:
        return pl.pallas_call(
            functools.partial(_inproj_kernel, tn), grid=(m // tm,),
            in_specs=[x_spec, _resident((1, d)), _resident((d, n))], out_specs=o_spec,
            out_shape=jax.ShapeDtypeStruct((m, n), BF16),
            compiler_params=_params("parallel"),
            name="inproj",
        )(x2d, gain, w)
    na = w_aux.shape[1]
    return pl.pallas_call(
        functools.partial(_inproj_aux_kernel, tn), grid=(m // tm,),
        in_specs=[x_spec, _resident((1, d)), _resident((d, n)), _resident((d, na))],
        out_specs=[o_spec, pl.BlockSpec((tm, na), lambda i: (i, 0))],
        out_shape=[jax.ShapeDtypeStruct((m, n), BF16), jax.ShapeDtypeStruct((m, na), F32)],
        compiler_params=_params("parallel"),
        name="inproj_aux",
    )(x2d, gain, w, w_aux)


def _outproj_mlp_kernel(tf, x_ref, y_ref, wo_ref, g_ref, wu_ref, wd_ref, o_ref):
    xn = x_ref[...] + jnp.dot(y_ref[...], wo_ref[...], preferred_element_type=F32)
    o_ref[...] = xn
    h = _rms(xn, g_ref[...]).astype(BF16)
    for j in range(wu_ref.shape[1] // tf):
        a = jnp.dot(h, wu_ref[:, j * tf:(j + 1) * tf], preferred_element_type=F32)
        a = jnp.maximum(a, 0.0)
        a = (a * a).astype(BF16)
        o_ref[...] += jnp.dot(a, wd_ref[j * tf:(j + 1) * tf, :], preferred_element_type=F32)


def _outproj_mlp(x2d, y, w_out, gain, w_up, w_down, tf=512):
    m, d = x2d.shape
    ky = y.shape[1]
    ff = w_up.shape[1]
    tm = min(m, PROJ_ROWS)
    return pl.pallas_call(
        functools.partial(_outproj_mlp_kernel, tf),
        grid=(m // tm,),
        in_specs=[pl.BlockSpec((tm, d), lambda i: (i, 0)),
                  pl.BlockSpec((tm, ky), lambda i: (i, 0)),
                  _resident((ky, d)), _resident((1, d)), _resident((d, ff)), _resident((ff, d))],
        out_specs=pl.BlockSpec((tm, d), lambda i: (i, 0)),
        out_shape=jax.ShapeDtypeStruct((m, d), F32),
        compiler_params=_params("parallel"),
        name="outproj_mlp",
    )(x2d, y, w_out, gain, w_up, w_down)


def _ret_constants():
    c = RET_CHUNK
    idx = np.arange(c, dtype=np.float64)
    log_gamma = np.log1p(-(2.0 ** (-5.0 - np.arange(RET_HEADS, dtype=np.float64))))
    diff = idx[:, None] - idx[None, :]
    scale = RET_DK ** -0.5
    dmat = np.where(diff >= 0, np.exp(log_gamma[:, None, None] * np.maximum(diff, 0.0)), 0.0) * scale
    qd = np.exp(log_gamma[:, None] * (idx + 1.0))
    kd = np.exp(log_gamma[:, None] * (c - 1.0 - idx)) * scale
    qd = np.broadcast_to(qd[:, :, None], (RET_HEADS, c, RET_DK))
    kd = np.broadcast_to(kd[:, :, None], (RET_HEADS, c, RET_DK))
    chunk_decay = [float(v) for v in np.exp(log_gamma * c)]
    return (jnp.asarray(dmat, F32), jnp.asarray(qd, F32), jnp.asarray(kd, F32), chunk_decay)


def _ret_kernel(chunk_decay, q_ref, k_ref, v_ref, g_ref, cos_ref, sin_ref, dmat_ref, qd_ref, kd_ref,
                o_ref, s_ref):
    @pl.when(pl.program_id(1) == 0)
    def _():
        s_ref[...] = jnp.zeros_like(s_ref)

    cos = cos_ref[...]
    sin = sin_ref[...]
    half = RET_DK // 2

    def rope(t):
        return t * cos + jnp.concatenate([t[:, half:], t[:, :half]], axis=1) * sin

    heads = range(RET_HEADS)
    dot = functools.partial(jnp.dot, preferred_element_type=F32)
    qr = [rope(q_ref[:, h * RET_DK:(h + 1) * RET_DK].astype(F32)) for h in heads]
    kr = [rope(k_ref[:, h * RET_DK:(h + 1) * RET_DK].astype(F32)) for h in heads]
    v = [v_ref[:, h * RET_DV:(h + 1) * RET_DV] for h in heads]
    scores = [lax.dot_general(qr[h].astype(BF16), kr[h].astype(BF16), NT_DIMS, preferred_element_type=F32)
              * dmat_ref[h] for h in heads]
    state = [s_ref[h] for h in heads]
    o_cross = [dot((qr[h] * qd_ref[h]).astype(BF16), state[h].astype(BF16)) for h in heads]
    o = [o_cross[h] + dot(scores[h].astype(BF16), v[h]) for h in heads]
    for h in heads:
        kdec = (kr[h] * kd_ref[h]).astype(BF16)
        s_ref[h] = state[h] * chunk_decay[h] + lax.dot_general(kdec, v[h], TN_DIMS, preferred_element_type=F32)
    for h in heads:
        mu = jnp.mean(o[h], axis=-1, keepdims=True)
        oc = o[h] - mu
        var = jnp.mean(oc * oc, axis=-1, keepdims=True)
        g = g_ref[:, h * RET_DV:(h + 1) * RET_DV].astype(F32)
        y = oc * lax.rsqrt(var + NORM_EPS) * (g * _sigmoid(g))
        o_ref[:, h * RET_DV:(h + 1) * RET_DV] = y.astype(o_ref.dtype)


def _retention_core(qkvg, cos_t, sin_t, b, s):
    c = RET_CHUNK
    n = s // c
    m = b * s
    dmat, qd, kd, chunk_decay = _ret_constants()
    row = lambda bi, ni: bi * n + ni
    const3 = lambda bi, ni: (0, 0, 0)
    return pl.pallas_call(
        functools.partial(_ret_kernel, chunk_decay),
        grid=(b, n),
        in_specs=[pl.BlockSpec((c, RET_QK), lambda bi, ni: (row(bi, ni), 0)),
                  pl.BlockSpec((c, RET_QK), lambda bi, ni: (row(bi, ni), 1)),
                  pl.BlockSpec((c, RET_V), lambda bi, ni: (row(bi, ni), 1)),
                  pl.BlockSpec((c, RET_V), lambda bi, ni: (row(bi, ni), 2)),
                  pl.BlockSpec((c, RET_DK), lambda bi, ni: (row(bi, ni), 0)),
                  pl.BlockSpec((c, RET_DK), lambda bi, ni: (row(bi, ni), 0)),
                  pl.BlockSpec((RET_HEADS, c, c), const3),
                  pl.BlockSpec((RET_HEADS, c, RET_DK), const3),
                  pl.BlockSpec((RET_HEADS, c, RET_DK), const3)],
        out_specs=pl.BlockSpec((c, RET_V), lambda bi, ni: (row(bi, ni), 0)),
        out_shape=jax.ShapeDtypeStruct((m, RET_V), BF16),
        scratch_shapes=[pltpu.VMEM((RET_HEADS, RET_DK, RET_DV), F32)],
        compiler_params=_params("parallel", "arbitrary"),
        name="retention_core",
    )(qkvg, qkvg, qkvg, qkvg, cos_t, sin_t, dmat, qd, kd)


def _swa_kernel(sink_ref, q_ref, kv_ref, cos_ref, sin_ref, qg_ref, kg_ref, o_ref, kp_ref, vp_ref):
    w = SWA_WINDOW
    dh = SWA_HEAD_DIM
    ni = pl.program_id(1)

    @pl.when(ni == 0)
    def _():
        kp_ref[...] = jnp.zeros_like(kp_ref)
        vp_ref[...] = jnp.zeros_like(vp_ref)

    cos = cos_ref[...]
    sin = sin_ref[...]

    def rope(t):
        width = t.shape[1]
        reps = width // LANES
        c = jnp.concatenate([cos] * reps, axis=1)
        s = jnp.concatenate([sin] * reps, axis=1)
        lane = lax.broadcasted_iota(jnp.int32, t.shape, 1)
        first = (lane % dh) < (dh // 2)
        partner = jnp.where(first, pltpu.roll(t, width - dh // 2, 1), pltpu.roll(t, dh // 2, 1))
        return t * c + partner * s

    def inv_rms(sq, h):
        ss = jnp.sum(sq[:, h * dh:(h + 1) * dh], axis=-1, keepdims=True)
        return lax.rsqrt(ss * (1.0 / dh) + NORM_EPS)

    qf = q_ref[...].astype(F32)
    q_sq = qf * qf
    q_rot = rope(qf * qg_ref[...])
    kf = kv_ref[:, :SWA_KV].astype(F32)
    k_sq = kf * kf
    k_rot = rope(kf * kg_ref[...])
    v_cur = kv_ref[:, SWA_KV:]

    rows = SWA_GROUP * w
    trow = lax.broadcasted_iota(jnp.int32, (rows, 2 * w), 0) % w
    col = lax.broadcasted_iota(jnp.int32, (rows, 2 * w), 1)
    rel = trow + w - col
    first_key = jnp.where(ni > 0, 0, w)
    valid = (rel >= 0) & (rel < w) & (col >= first_key)

    kv_heads = range(SWA_KV_HEADS)
    k_new = [(k_rot[:, kh * dh:(kh + 1) * dh] * inv_rms(k_sq, kh)).astype(BF16) for kh in kv_heads]
    k2 = [jnp.concatenate([kp_ref[:, kh * dh:(kh + 1) * dh], k_new[kh]], axis=0) for kh in kv_heads]
    v2 = [jnp.concatenate([vp_ref[:, kh * dh:(kh + 1) * dh], v_cur[:, kh * dh:(kh + 1) * dh]], axis=0)
          for kh in kv_heads]
    q_st, scale, sink = [], [], []
    for kh in kv_heads:
        hs = [kh * SWA_GROUP + gi for gi in range(SWA_GROUP)]
        q_st.append(jnp.concatenate([q_rot[:, h * dh:(h + 1) * dh] for h in hs], axis=0).astype(BF16))
        scale.append(jnp.concatenate([inv_rms(q_sq, h) for h in hs], axis=0) * (dh ** -0.5))
        sink.append(jnp.concatenate([jnp.full((w, 1), sink_ref[h], F32) for h in hs], axis=0))
    sc = [lax.dot_general(q_st[kh], k2[kh], NT_DIMS, preferred_element_type=F32) for kh in kv_heads]
    sc = [jnp.where(valid, sc[kh] * scale[kh], -jnp.inf) for kh in kv_heads]
    mx = [jnp.maximum(jnp.max(sc[kh], axis=-1, keepdims=True), sink[kh]) for kh in kv_heads]
    p = [jnp.exp(sc[kh] - mx[kh]) for kh in kv_heads]
    den = [jnp.sum(p[kh], axis=-1, keepdims=True) + jnp.exp(sink[kh] - mx[kh]) for kh in kv_heads]
    o_st = [jnp.dot(p[kh].astype(BF16), v2[kh], preferred_element_type=F32) / den[kh] for kh in kv_heads]
    outs = [o_st[kh][gi * w:(gi + 1) * w, :] for kh in kv_heads for gi in range(SWA_GROUP)]
    o_ref[...] = jnp.concatenate(outs, axis=1).astype(o_ref.dtype)
    kp_ref[...] = jnp.concatenate(k_new, axis=1)
    vp_ref[...] = v_cur


def _swa_core(qkv, cos_t, sin_t, q_gain, k_gain, sinks, b, s):
    w = SWA_WINDOW
    n = s // w
    m = b * s
    qg = jnp.tile(q_gain.astype(F32), SWA_HEADS)[None, :]
    kg = jnp.tile(k_gain.astype(F32), SWA_KV_HEADS)[None, :]
    row = lambda bi, ni: bi * n + ni
    return pl.pallas_call(
        _swa_kernel,
        grid=(b, n),
        in_specs=[pl.BlockSpec(memory_space=pltpu.SMEM),
                  pl.BlockSpec((w, SWA_Q), lambda bi, ni: (row(bi, ni), 0)),
                  pl.BlockSpec((w, 2 * SWA_KV), lambda bi, ni: (row(bi, ni), SWA_Q // (2 * SWA_KV))),
                  pl.BlockSpec((w, LANES), lambda bi, ni: (row(bi, ni), 0)),
                  pl.BlockSpec((w, LANES), lambda bi, ni: (row(bi, ni), 0)),
                  pl.BlockSpec((1, SWA_Q), lambda bi, ni: (0, 0)),
                  pl.BlockSpec((1, SWA_KV), lambda bi, ni: (0, 0))],
        out_specs=pl.BlockSpec((w, SWA_Q), lambda bi, ni: (row(bi, ni), 0)),
        out_shape=jax.ShapeDtypeStruct((m, SWA_Q), BF16),
        scratch_shapes=[pltpu.VMEM((w, SWA_KV), BF16), pltpu.VMEM((w, SWA_KV), BF16)],
        compiler_params=_params("parallel", "arbitrary"),
        name="swa_core",
    )(sinks.astype(F32), qkv, qkv, cos_t, sin_t, qg, kg)


GDN_HALO = 8
GDN_STACK = 4
GDN_ROWS = GDN_STACK * GDN_CHUNK
GDN_STEP_CHUNKS = 2


def _gdn_masks():
    i = np.arange(GDN_CHUNK)[:, None]
    j = np.arange(GDN_ROWS)[None, :] % GDN_CHUNK
    r = np.arange(GDN_ROWS)
    same = (r[:, None] // GDN_CHUNK) == (r[None, :] // GDN_CHUNK)
    return jnp.asarray(i >= j, F32), jnp.asarray(i > j, F32), jnp.asarray(same, BF16)


def _gdn_kernel(qkv_ref, z_ref, ba_ref, cw_ref, alog_ref, dtb_ref, og_ref, mt_ref, ms_ref, bd_ref, o_ref,
                ext_ref, q_scr, k_scr, v_scr, s_ref):
    c = GDN_CHUNK
    dh = GDN_HEAD_DIM
    nh = GDN_V_HEADS
    rows = GDN_STEP_CHUNKS * c

    @pl.when(pl.program_id(1) == 0)
    def _():
        s_ref[...] = jnp.zeros_like(s_ref)
        ext_ref[0:GDN_HALO, :] = jnp.zeros((GDN_HALO, GDN_CONV_DIM), F32)

    ext_ref[GDN_HALO:GDN_HALO + rows, :] = qkv_ref[...].astype(F32)

    def conv_block(r0, j):
        tok = slice(r0, r0 + c)
        cs = slice(j * dh, (j + 1) * dh)
        e0 = GDN_HALO + r0
        acc = ext_ref[e0:e0 + c, cs] * cw_ref[GDN_CONV - 1:GDN_CONV, cs]
        for d in range(1, GDN_CONV):
            acc = acc + ext_ref[e0 - d:e0 - d + c, cs] * cw_ref[GDN_CONV - 1 - d:GDN_CONV - d, cs]
        act = acc * _sigmoid(acc)
        if j < 2 * GDN_QK_HEADS:
            nrm = act * lax.rsqrt(jnp.sum(act * act, axis=-1, keepdims=True) + NORM_EPS)
            if j < GDN_QK_HEADS:
                q_scr[tok, cs] = nrm * (dh ** -0.5)
            else:
                k_scr[tok, j * dh - GDN_QK:(j + 1) * dh - GDN_QK] = nrm
        else:
            v_scr[tok, j * dh - 2 * GDN_QK:(j + 1) * dh - 2 * GDN_QK] = act

    ri = lax.broadcasted_iota(jnp.int32, (c, c), 0)
    ci = lax.broadcasted_iota(jnp.int32, (c, c), 1)
    cum = (ri >= ci).astype(F32)
    m_tril = mt_ref[...]
    m_strict = ms_ref[...]
    eye = m_tril - m_strict
    left = lax.broadcasted_iota(jnp.int32, (c, dh), 1) < c
    groups = range(nh // GDN_STACK)
    order = [[GDN_STACK * grp + r for r in (0, 2, 1, 3)] for grp in groups]
    steps = int(math.log2(c)) - 1
    dot = functools.partial(jnp.dot, preferred_element_type=F32)

    def block_diag(packed):
        return jnp.concatenate([packed] * GDN_STACK, axis=0) * bd_ref[...]

    def pack_pairs(x):
        tile = jnp.where(left, x[:c, :], x[c:, :])
        return jnp.concatenate([tile, tile], axis=1)

    def pack_cols(cols, grp):
        h0, h1, h2, h3 = order[grp]
        return jnp.concatenate([jnp.where(left, cols[h0], cols[h1]), jnp.where(left, cols[h2], cols[h3])], axis=1)

    def chunk(r0):
        tok = slice(r0, r0 + c)
        for j in range(GDN_CONV_DIM // dh):
            conv_block(r0, j)
        ba = ba_ref[tok, :]
        beta = pltpu.roll(_sigmoid(ba), nh, 1)
        pre = ba + dtb_ref[...]
        softplus = jnp.maximum(pre, 0.0) + jnp.log1p(jnp.exp(-jnp.abs(pre)))
        g = -jnp.exp(alog_ref[...]) * softplus
        gc = jnp.dot(cum, g, preferred_element_type=F32, precision=lax.Precision.HIGHEST)
        gc_t = gc.T
        g_col = [jnp.broadcast_to(gc[:, nh + hv:nh + hv + 1], (c, dh)) for hv in range(nh)]
        b_col = [jnp.broadcast_to(beta[:, nh + hv:nh + hv + 1], (c, dh)) for hv in range(nh)]
        eg = [jnp.exp(g_col[hv]) for hv in range(nh)]
        g_last = [g_col[hv][c - 1:c, :] for hv in range(nh)]
        q_h = [q_scr[tok, hq * dh:(hq + 1) * dh] for hq in range(GDN_QK_HEADS)]
        k_h = [k_scr[tok, hq * dh:(hq + 1) * dh] for hq in range(GDN_QK_HEADS)]

        low, attn = [], []
        for grp in groups:
            kab = jnp.concatenate([k_h[2 * grp], k_h[2 * grp + 1]], axis=0).astype(BF16)
            qab = jnp.concatenate([q_h[2 * grp], q_h[2 * grp + 1]], axis=0).astype(BF16)
            kk = pack_pairs(lax.dot_general(kab, kab, NT_DIMS, preferred_element_type=F32))
            qk = pack_pairs(lax.dot_general(qab, kab, NT_DIMS, preferred_element_type=F32))
            g_rows = jnp.concatenate([gc_t[nh + hv:nh + hv + 1, :] for hv in order[grp]], axis=1)
            decay = jnp.exp((pack_cols(g_col, grp) - g_rows) * m_tril) * m_tril
            low.append(kk * decay * pack_cols(b_col, grp) * m_strict)
            attn.append((qk * decay).astype(BF16))

        t_inv = [eye - low[grp] for grp in groups]
        pw = [low[grp].astype(BF16) for grp in groups]
        pw = [dot(pw[grp], block_diag(pw[grp])).astype(BF16) for grp in groups]
        for it in range(steps):
            rhs = [block_diag(pw[grp]) for grp in groups]
            if it + 1 < steps:
                both = [dot(jnp.concatenate([pw[grp], t_inv[grp].astype(BF16)], axis=0), rhs[grp]) for grp in groups]
                pw = [both[grp][:c].astype(BF16) for grp in groups]
                t_inv = [t_inv[grp] + both[grp][c:] for grp in groups]
            else:
                t_inv = [t_inv[grp] + dot(t_inv[grp].astype(BF16), rhs[grp]) for grp in groups]

        uw = []
        for grp in groups:
            rhs = jnp.concatenate(
                [jnp.concatenate([v_scr[tok, hv * dh:(hv + 1) * dh] * b_col[hv],
                                  k_h[hv // GDN_REP] * (b_col[hv] * eg[hv])], axis=1) for hv in order[grp]],
                axis=0).astype(BF16)
            uw.append(dot(block_diag(t_inv[grp].astype(BF16)), rhs))

        states = [s_ref[hv] for hv in range(nh)]
        sbs = [st.astype(BF16) for st in states]
        vnb = []
        for grp in groups:
            v_new = jnp.concatenate(
                [uw[grp][r * c:(r + 1) * c, :dh] - dot(uw[grp][r * c:(r + 1) * c, dh:].astype(BF16), sbs[hv])
                 for r, hv in enumerate(order[grp])], axis=0)
            vnb.append(v_new.astype(BF16))
        o_intra = [dot(block_diag(attn[grp]), vnb[grp]) for grp in groups]
        for grp in groups:
            for r, hv in enumerate(order[grp]):
                blk = slice(r * c, (r + 1) * c)
                q_in = (q_h[hv // GDN_REP] * eg[hv]).astype(BF16)
                k_out = (k_h[hv // GDN_REP] * jnp.exp(g_last[hv] - g_col[hv])).astype(BF16)
                o = o_intra[grp][blk] + dot(q_in, sbs[hv])
                s_ref[hv] = states[hv] * jnp.exp(g_last[hv]) + lax.dot_general(
                    k_out, vnb[grp][blk], TN_DIMS, preferred_element_type=F32)
                zz = z_ref[tok, hv * dh:(hv + 1) * dh].astype(F32)
                o_ref[tok, hv * dh:(hv + 1) * dh] = (
                    _rms(o, og_ref[...]) * (zz * _sigmoid(zz))).astype(o_ref.dtype)

    for t in range(GDN_STEP_CHUNKS):
        chunk(t * c)
    ext_ref[0:GDN_HALO, :] = ext_ref[rows:rows + GDN_HALO, :]


def _gdn_core(qkvz, ba, conv_w, alog_row, dtb_row, out_gain, b, s):
    c = GDN_STEP_CHUNKS * GDN_CHUNK
    n = s // c
    m = b * s
    m_tril, m_strict, block_sel = _gdn_masks()
    row = lambda bi, ni: bi * n + ni
    const2 = lambda bi, ni: (0, 0)
    return pl.pallas_call(
        _gdn_kernel,
        grid=(b, n),
        in_specs=[pl.BlockSpec((c, GDN_CONV_DIM), lambda bi, ni: (row(bi, ni), 0)),
                  pl.BlockSpec((c, GDN_V), lambda bi, ni: (row(bi, ni), GDN_CONV_DIM // GDN_V)),
                  pl.BlockSpec((c, LANES), lambda bi, ni: (row(bi, ni), 0)),
                  pl.BlockSpec((GDN_CONV, GDN_CONV_DIM), const2),
                  pl.BlockSpec((1, LANES), const2),
                  pl.BlockSpec((1, LANES), const2),
                  pl.BlockSpec((1, GDN_HEAD_DIM), const2),
                  pl.BlockSpec((GDN_CHUNK, GDN_ROWS), const2),
                  pl.BlockSpec((GDN_CHUNK, GDN_ROWS), const2),
                  pl.BlockSpec((GDN_ROWS, GDN_ROWS), const2)],
        out_specs=pl.BlockSpec((c, GDN_V), lambda bi, ni: (row(bi, ni), 0)),
        out_shape=jax.ShapeDtypeStruct((m, GDN_V), BF16),
        scratch_shapes=[pltpu.VMEM((c + GDN_HALO, GDN_CONV_DIM), F32),
                        pltpu.VMEM((c, GDN_QK), F32),
                        pltpu.VMEM((c, GDN_QK), F32),
                        pltpu.VMEM((c, GDN_V), F32),
                        pltpu.VMEM((GDN_V_HEADS, GDN_HEAD_DIM, GDN_HEAD_DIM), F32)],
        compiler_params=_params("parallel", "arbitrary"),
        name="gdn_core",
    )(qkvz, qkvz, ba, conv_w, alog_row, dtb_row, out_gain, m_tril, m_strict, block_sel)


def kernel(x, positions, norm_mix, norm_mlp, w_up, w_down, ret_w_in, ret_w_out, swa_w_in, swa_q_gain,
           swa_k_gain, swa_sinks, swa_w_out, gdn_w_in, gdn_conv_w, gdn_a_log, gdn_dt_bias, gdn_out_gain,
           gdn_w_out):
    b, s, d = x.shape
    m = b * s
    depth = norm_mix.shape[0]
    x2d = x.reshape(m, d)
    pos_col = positions.reshape(m, 1).astype(F32)
    ret_cos, ret_sin = _rope_tables(pos_col, RET_DK, RET_DK)
    swa_cos, swa_sin = _rope_tables(pos_col, SWA_HEAD_DIM, LANES)

    for i in range(depth):
        mixer, j = i % 3, i // 3
        gain = norm_mix[i][None, :]
        if mixer == 0:
            proj = _inproj(x2d, gain, ret_w_in[j].astype(BF16), tn=1024)
            y = _retention_core(proj, ret_cos, ret_sin, b, s)
            w_out = ret_w_out[j]
        elif mixer == 1:
            proj = _inproj(x2d, gain, swa_w_in[j].astype(BF16), tn=SWA_Q + 2 * SWA_KV)
            y = _swa_core(proj, swa_cos, swa_sin, swa_q_gain[j], swa_k_gain[j], swa_sinks[j], b, s)
            w_out = swa_w_out[j]
        else:
            n_main = GDN_CONV_DIM + GDN_V
            w_main = gdn_w_in[j][:, :n_main].astype(BF16)
            w_gate = jnp.pad(gdn_w_in[j][:, n_main:], ((0, 0), (0, LANES - 2 * GDN_V_HEADS))).astype(BF16)
            proj, ba = _inproj(x2d, gain, w_main, tn=1024, w_aux=w_gate)
            lane_pad = (GDN_V_HEADS, LANES - 2 * GDN_V_HEADS)
            alog_row = jnp.pad(gdn_a_log[j].astype(F32), lane_pad)[None, :]
            dtb_row = jnp.pad(gdn_dt_bias[j].astype(F32), lane_pad)[None, :]
            y = _gdn_core(proj, ba, gdn_conv_w[j].astype(F32), alog_row, dtb_row,
                          gdn_out_gain[j].astype(F32)[None, :], b, s)
            w_out = gdn_w_out[j]
        x2d = _outproj_mlp(x2d, y, w_out.astype(BF16), norm_mlp[i][None, :],
                           w_up[i].astype(BF16), w_down[i].astype(BF16))
    return x2d.reshape(b, s, d)
```

```python
import functools
import math

import numpy as np
import jax
import jax.numpy as jnp
from jax import lax
from jax.experimental import pallas as pl
from jax.experimental.pallas import tpu as pltpu

F32 = jnp.float32
BF16 = jnp.bfloat16

D_MODEL = 1024
D_FF = 4 * D_MODEL
NORM_EPS = 1e-6
ROPE_THETA = 10000.0

RET_HEADS = 4
RET_DK = 256
RET_DV = 512
RET_QK = RET_HEADS * RET_DK
RET_V = RET_HEADS * RET_DV
RET_CHUNK = 256

SWA_HEADS = 16
SWA_KV_HEADS = 4
SWA_HEAD_DIM = 64
SWA_GROUP = SWA_HEADS // SWA_KV_HEADS
SWA_WINDOW = 128
SWA_Q = SWA_HEADS * SWA_HEAD_DIM
SWA_KV = SWA_KV_HEADS * SWA_HEAD_DIM

GDN_QK_HEADS = 8
GDN_V_HEADS = 16
GDN_HEAD_DIM = 128
GDN_QK = GDN_QK_HEADS * GDN_HEAD_DIM
GDN_V = GDN_V_HEADS * GDN_HEAD_DIM
GDN_CONV_DIM = 2 * GDN_QK + GDN_V
GDN_CONV = 4
GDN_CHUNK = 64
GDN_REP = GDN_V_HEADS // GDN_QK_HEADS

LANES = 128
VMEM_LIMIT = 56 * 1024 * 1024
PROJ_ROWS = 512

NT_DIMS = (((1,), (1,)), ((), ()))
TN_DIMS = (((0,), (0,)), ((), ()))


def _params(*sem):
    return pltpu.CompilerParams(dimension_semantics=sem, vmem_limit_bytes=VMEM_LIMIT)


def _sigmoid(x):
    return 1.0 / (1.0 + jnp.exp(-x))


def _rms(x, gain_row):
    ms = jnp.mean(x * x, axis=-1, keepdims=True)
    return x * lax.rsqrt(ms + NORM_EPS) * gain_row


def _rope_table_kernel(pos_ref, inv_ref, sign_ref, cos_ref, sin_ref):
    ang = pos_ref[...] * inv_ref[...]
    cos_ref[...] = jnp.cos(ang)
    sin_ref[...] = jnp.sin(ang) * sign_ref[...]


def _rope_tables(pos_col, head_dim, width):
    m = pos_col.shape[0]
    half = head_dim // 2
    inv = ROPE_THETA ** (-np.arange(0, head_dim, 2, dtype=np.float32) / head_dim)
    reps = width // head_dim
    inv_row = np.tile(np.concatenate([inv, inv]), reps)[None, :].astype(np.float32)
    sign_row = np.tile(np.concatenate([-np.ones(half), np.ones(half)]), reps)[None, :].astype(np.float32)
    tm = min(m, 1024)
    return pl.pallas_call(
        _rope_table_kernel,
        grid=(m // tm,),
        in_specs=[pl.BlockSpec((tm, 1), lambda i: (i, 0)),
                  pl.BlockSpec((1, width), lambda i: (0, 0)),
                  pl.BlockSpec((1, width), lambda i: (0, 0))],
        out_specs=[pl.BlockSpec((tm, width), lambda i: (i, 0)),
                   pl.BlockSpec((tm, width), lambda i: (i, 0))],
        out_shape=[jax.ShapeDtypeStruct((m, width), F32)] * 2,
        compiler_params=_params("parallel"),
        name="rope_tables",
    )(pos_col, jnp.asarray(inv_row), jnp.asarray(sign_row))


def _resident(shape):
    return pl.BlockSpec(shape, lambda i: (0,) * len(shape), pipeline_mode=pl.Buffered(1))


def _inproj_kernel(tn, x_ref, g_ref, w_ref, o_ref):
    h = _rms(x_ref[...], g_ref[...]).astype(BF16)
    for j in range(o_ref.shape[1] // tn):
        cols = slice(j * tn, (j + 1) * tn)
        o_ref[:, cols] = jnp.dot(h, w_ref[:, cols], preferred_element_type=F32).astype(o_ref.dtype)


def _inproj_aux_kernel(tn, x_ref, g_ref, w_ref, wa_ref, o_ref, oa_ref):
    h = _rms(x_ref[...], g_ref[...]).astype(BF16)
    oa_ref[...] = jnp.dot(h, wa_ref[...], preferred_element_type=F32)
    for j in range(o_ref.shape[1] // tn):
        cols = slice(j * tn, (j + 1) * tn)
        o_ref[:, cols] = jnp.dot(h, w_ref[:, cols], preferred_element_type=F32).astype(o_ref.dtype)


def _inproj(x2d, gain, w, tn, w_aux=None):
    m, d = x2d.shape
    n = w.shape[1]
    tm = min(m, PROJ_ROWS)
    x_spec = pl.BlockSpec((tm, d), lambda i: (i, 0))
    o_spec = pl.BlockSpec((tm, n), lambda i: (i, 0))
    if w_aux is None:
        return pl.pallas_call(
            functools.partial(_inproj_kernel, tn), grid=(m // tm,),
            in_specs=[x_spec, _resident((1, d)), _resident((d, n))], out_specs=o_spec,
            out_shape=jax.ShapeDtypeStruct((m, n), BF16),
            compiler_params=_params("parallel"),
            name="inproj",
        )(x2d, gain, w)
    na = w_aux.shape[1]
    return pl.pallas_call(
        functools.partial(_inproj_aux_kernel, tn), grid=(m // tm,),
        in_specs=[x_spec, _resident((1, d)), _resident((d, n)), _resident((d, na))],
        out_specs=[o_spec, pl.BlockSpec((tm, na), lambda i: (i, 0))],
        out_shape=[jax.ShapeDtypeStruct((m, n), BF16), jax.ShapeDtypeStruct((m, na), F32)],
        compiler_params=_params("parallel"),
        name="inproj_aux",
    )(x2d, gain, w, w_aux)


def _ret_gate(p, o_ref, g_ref):
    cols = slice(p * RET_DV, (p + 1) * RET_DV)
    o = o_ref[:, cols].astype(F32)
    mu = jnp.mean(o, axis=-1, keepdims=True)
    oc = o - mu
    var = jnp.mean(oc * oc, axis=-1, keepdims=True)
    g = g_ref[:, cols].astype(F32)
    return cols, (oc * lax.rsqrt(var + NORM_EPS) * (g * _sigmoid(g))).astype(BF16)


def _mlp_chunks(tf, h, wu_ref, wd_ref, o_ref, after_chunk=None):
    for j in range(wu_ref.shape[1] // tf):
        a = jnp.dot(h, wu_ref[:, j * tf:(j + 1) * tf], preferred_element_type=F32)
        a = jnp.maximum(a, 0.0)
        a = (a * a).astype(BF16)
        o_ref[...] += jnp.dot(a, wd_ref[j * tf:(j + 1) * tf, :], preferred_element_type=F32)
        if after_chunk is not None:
            after_chunk(j)


def _outproj_mlp_kernel(tf, x_ref, y_ref, wo_ref, g_ref, wu_ref, wd_ref, o_ref):
    o_ref[...] = x_ref[...] + jnp.dot(y_ref[...], wo_ref[...], preferred_element_type=F32)
    _mlp_chunks(tf, _rms(o_ref[...], g_ref[...]).astype(BF16), wu_ref, wd_ref, o_ref)


def _gated_outproj_mlp_kernel(tf, gate, n_pieces, n_gate, x_ref, *refs):
    gate_refs, (wo_ref, g_ref, wu_ref, wd_ref, o_ref, x1_buf) = refs[:n_gate], refs[n_gate:]

    @pl.when(pl.program_id(0) == 0)
    def _():
        x1_buf[...] = jnp.zeros_like(x1_buf)

    x1 = x1_buf[...]
    o_ref[...] = x1
    every = (wu_ref.shape[1] // tf) // n_pieces

    def outproj_piece(j):
        if (j + 1) % every:
            return
        p = j // every
        cols, y = gate(p, *gate_refs)
        part = jnp.dot(y, wo_ref[cols, :], preferred_element_type=F32)
        x1_buf[...] = (x_ref[...] if p == 0 else x1_buf[...]) + part

    _mlp_chunks(tf, _rms(x1, g_ref[...]).astype(BF16), wu_ref, wd_ref, o_ref, outproj_piece)


def _outproj_mlp(x2d, y, w_out, gain, w_up, w_down, gate=None, gate_pieces=1, gate_args=(), gate_specs=(),
                 tf=512):
    m, d = x2d.shape
    ky = y.shape[1]
    ff = w_up.shape[1]
    tm = min(m, PROJ_ROWS)
    nt = m // tm
    weights = [_resident((ky, d)), _resident((1, d)), _resident((d, ff)), _resident((ff, d))]
    if gate is None:
        return pl.pallas_call(
            functools.partial(_outproj_mlp_kernel, tf),
            grid=(nt,),
            in_specs=[pl.BlockSpec((tm, d), lambda i: (i, 0)), pl.BlockSpec((tm, ky), lambda i: (i, 0)), *weights],
            out_specs=pl.BlockSpec((tm, d), lambda i: (i, 0)),
            out_shape=jax.ShapeDtypeStruct((m, d), F32),
            compiler_params=_params("parallel"),
            name="outproj_mlp",
        )(x2d, y, w_out, gain, w_up, w_down)
    done = lambda i: jnp.maximum(i - 1, 0)
    ahead = lambda i: jnp.minimum(i, nt - 1)
    return pl.pallas_call(
        functools.partial(_gated_outproj_mlp_kernel, tf, gate, gate_pieces, 1 + len(gate_args)),
        grid=(nt + 1,),
        in_specs=[pl.BlockSpec((tm, d), lambda i: (ahead(i), 0)),
                  pl.BlockSpec((tm, ky), lambda i: (ahead(i), 0)),
                  *gate_specs(tm, ahead), *weights],
        out_specs=pl.BlockSpec((tm, d), lambda i: (done(i), 0)),
        out_shape=jax.ShapeDtypeStruct((m, d), F32),
        scratch_shapes=[pltpu.VMEM((tm, d), F32)],
        compiler_params=_params("arbitrary"),
        name="gated_outproj_mlp",
    )(x2d, y, *gate_args, w_out, gain, w_up, w_down)


def _ret_constants():
    c = RET_CHUNK
    idx = np.arange(c, dtype=np.float64)
    log_gamma = np.log1p(-(2.0 ** (-5.0 - np.arange(RET_HEADS, dtype=np.float64))))
    diff = idx[:, None] - idx[None, :]
    scale = RET_DK ** -0.5
    dmat = np.where(diff >= 0, np.exp(log_gamma[:, None, None] * np.maximum(diff, 0.0)), 0.0) * scale
    qd = np.exp(log_gamma[:, None] * (idx + 1.0))
    kd = np.exp(log_gamma[:, None] * (c - 1.0 - idx)) * scale
    qd = np.broadcast_to(qd[:, :, None], (RET_HEADS, c, RET_DK))
    kd = np.broadcast_to(kd[:, :, None], (RET_HEADS, c, RET_DK))
    chunk_decay = [float(v) for v in np.exp(log_gamma * c)]
    return (jnp.asarray(dmat, F32), jnp.asarray(qd, F32), jnp.asarray(kd, F32), chunk_decay)


def _ret_kernel(chunk_decay, q_ref, k_ref, v_ref, cos_ref, sin_ref, dmat_ref, qd_ref, kd_ref,
                o_ref, s_ref):
    @pl.when(pl.program_id(1) == 0)
    def _():
        s_ref[...] = jnp.zeros_like(s_ref)

    cos = cos_ref[...]
    sin = sin_ref[...]
    half = RET_DK // 2

    def rope(t):
        return t * cos + jnp.concatenate([t[:, half:], t[:, :half]], axis=1) * sin

    heads = range(RET_HEADS)
    dot = functools.partial(jnp.dot, preferred_element_type=F32)
    qr = [rope(q_ref[:, h * RET_DK:(h + 1) * RET_DK].astype(F32)) for h in heads]
    kr = [rope(k_ref[:, h * RET_DK:(h + 1) * RET_DK].astype(F32)) for h in heads]
    v = [v_ref[:, h * RET_DV:(h + 1) * RET_DV] for h in heads]
    scores = [lax.dot_general(qr[h].astype(BF16), kr[h].astype(BF16), NT_DIMS, preferred_element_type=F32)
              * dmat_ref[h] for h in heads]
    state = [s_ref[h] for h in heads]
    o_cross = [dot((qr[h] * qd_ref[h]).astype(BF16), state[h].astype(BF16)) for h in heads]
    o = [o_cross[h] + dot(scores[h].astype(BF16), v[h]) for h in heads]
    for h in heads:
        kdec = (kr[h] * kd_ref[h]).astype(BF16)
        s_ref[h] = state[h] * chunk_decay[h] + lax.dot_general(kdec, v[h], TN_DIMS, preferred_element_type=F32)
    for h in heads:
        o_ref[:, h * RET_DV:(h + 1) * RET_DV] = o[h].astype(o_ref.dtype)


def _retention_core(qkvg, cos_t, sin_t, b, s):
    c = RET_CHUNK
    n = s // c
    m = b * s
    dmat, qd, kd, chunk_decay = _ret_constants()
    row = lambda bi, ni: bi * n + ni
    const3 = lambda bi, ni: (0, 0, 0)
    return pl.pallas_call(
        functools.partial(_ret_kernel, chunk_decay),
        grid=(b, n),
        in_specs=[pl.BlockSpec((c, RET_QK), lambda bi, ni: (row(bi, ni), 0)),
                  pl.BlockSpec((c, RET_QK), lambda bi, ni: (row(bi, ni), 1)),
                  pl.BlockSpec((c, RET_V), lambda bi, ni: (row(bi, ni), 1)),
                  pl.BlockSpec((c, RET_DK), lambda bi, ni: (row(bi, ni), 0)),
                  pl.BlockSpec((c, RET_DK), lambda bi, ni: (row(bi, ni), 0)),
                  pl.BlockSpec((RET_HEADS, c, c), const3),
                  pl.BlockSpec((RET_HEADS, c, RET_DK), const3),
                  pl.BlockSpec((RET_HEADS, c, RET_DK), const3)],
        out_specs=pl.BlockSpec((c, RET_V), lambda bi, ni: (row(bi, ni), 0)),
        out_shape=jax.ShapeDtypeStruct((m, RET_V), BF16),
        scratch_shapes=[pltpu.VMEM((RET_HEADS, RET_DK, RET_DV), F32)],
        compiler_params=_params("parallel", "arbitrary"),
        name="retention_core",
    )(qkvg, qkvg, qkvg, cos_t, sin_t, dmat, qd, kd)


def _swa_constants():
    dh, half = SWA_HEAD_DIM, SWA_HEAD_DIM // 2
    i = np.arange(2 * LANES)
    swap = (i[:, None] == ((i[None, :] // dh) * dh + (i[None, :] % dh + half) % dh))
    seg_q = (np.arange(SWA_Q)[:, None] // dh) == np.arange(LANES)[None, :]
    seg_k = (np.arange(SWA_KV)[:, None] // dh) == np.arange(LANES)[None, :]
    bf = lambda a: jnp.asarray(a, BF16)
    return bf(swap), bf(seg_q), bf(seg_k), bf(seg_q.T * (dh ** -0.5)), bf(seg_k.T)


def _swa_kernel(sink_ref, q_ref, kv_ref, cos_ref, sin_ref, qg_ref, qgs_ref, kg_ref, kgs_ref,
                swap_ref, segq_ref, segk_ref, expq_ref, expk_ref, o_ref, k_prev, v_prev):
    w = SWA_WINDOW
    ni = pl.program_id(1)
    dot = functools.partial(jnp.dot, preferred_element_type=F32)

    @pl.when(ni == 0)
    def _():
        k_prev[...] = jnp.zeros_like(k_prev)
        v_prev[...] = jnp.zeros_like(v_prev)

    cos = cos_ref[...]
    sin = sin_ref[...]

    def normed_rope(xb, gain_ref, gain_swapped_ref, seg_ref, exp_ref):
        width = xb.shape[1]
        reps = width // LANES
        xf = xb.astype(F32)
        partner = jnp.concatenate(
            [dot(xb[:, t * 2 * LANES:(t + 1) * 2 * LANES], swap_ref[...]) for t in range(reps // 2)], axis=1)
        rot = (xf * (jnp.concatenate([cos] * reps, axis=1) * gain_ref[...])
               + partner * (jnp.concatenate([sin] * reps, axis=1) * gain_swapped_ref[...]))
        ss = dot((xf * xf).astype(BF16), seg_ref[...])
        inv = lax.rsqrt(ss * (1.0 / SWA_HEAD_DIM) + NORM_EPS)
        inv_hi = inv.astype(BF16)
        inv_lo = (inv - inv_hi.astype(F32)).astype(BF16)
        return rot * (dot(inv_hi, exp_ref[...]) + dot(inv_lo, exp_ref[...]))

    q_n = normed_rope(q_ref[...], qg_ref, qgs_ref, segq_ref, expq_ref).astype(BF16)
    k_n = normed_rope(kv_ref[:, :SWA_KV], kg_ref, kgs_ref, segk_ref, expk_ref)
    v_f = kv_ref[:, SWA_KV:].astype(F32)

    left = lax.broadcasted_iota(jnp.int32, (w, LANES), 1) < SWA_HEAD_DIM

    def placements(x):
        out = []
        for kh in range(SWA_KV_HEADS):
            tile = x[:, (kh // 2) * LANES:(kh // 2 + 1) * LANES]
            moved = pltpu.roll(tile, SWA_HEAD_DIM, 1)
            in_left, in_right = (tile, moved) if kh % 2 == 0 else (moved, tile)
            out.append([jnp.where(left, in_left, 0.0).astype(BF16), jnp.where(left, 0.0, in_right).astype(BF16)])
        return out

    k_cur = placements(k_n)
    v_cur = placements(v_f)

    key = lax.broadcasted_iota(jnp.int32, (2 * w, 2 * w), 0)
    qry = lax.broadcasted_iota(jnp.int32, (2 * w, 2 * w), 1) % w
    rel = qry + w - key
    first_key = jnp.where(ni > 0, 0, w)
    valid = (rel >= 0) & (rel < w) & (key >= first_key)

    pairs = [(kh, sd) for kh in range(SWA_KV_HEADS) for sd in range(2)]
    sc, sink = {}, {}
    for kh, sd in pairs:
        slot = 2 * kh + sd
        k2 = jnp.concatenate([k_prev[slot], k_cur[kh][sd]], axis=0)
        q2 = jnp.concatenate([q_n[:, (2 * kh + t) * LANES:(2 * kh + t + 1) * LANES] for t in range(2)], axis=0)
        sc[kh, sd] = lax.dot_general(k2, q2, NT_DIMS, preferred_element_type=F32)
        heads = [SWA_GROUP * kh + 2 * t + sd for t in range(2)]
        sink[kh, sd] = jnp.concatenate([jnp.full((1, w), sink_ref[h], F32) for h in heads], axis=1)
    sc = {pr: jnp.where(valid, sc[pr], -jnp.inf) for pr in pairs}
    mx = {pr: jnp.maximum(jnp.max(sc[pr], axis=0, keepdims=True), sink[pr]) for pr in pairs}
    p = {pr: jnp.exp(sc[pr] - mx[pr]) for pr in pairs}
    den = {pr: jnp.sum(p[pr], axis=0, keepdims=True) + jnp.exp(sink[pr] - mx[pr]) for pr in pairs}
    p = {pr: (p[pr] * (1.0 / den[pr])).astype(BF16) for pr in pairs}
    for kh in range(SWA_KV_HEADS):
        v2 = jnp.concatenate([v_prev[2 * kh], v_cur[kh][0], v_prev[2 * kh + 1], v_cur[kh][1]], axis=0)
        p2 = jnp.concatenate([p[kh, 0], p[kh, 1]], axis=0)
        o_t = lax.dot_general(v2, p2, TN_DIMS, preferred_element_type=F32)
        for t in range(2):
            o_ref[:, (2 * kh + t) * LANES:(2 * kh + t + 1) * LANES] = o_t[:, t * w:(t + 1) * w].T.astype(o_ref.dtype)
    for kh, sd in pairs:
        k_prev[2 * kh + sd] = k_cur[kh][sd]
        v_prev[2 * kh + sd] = v_cur[kh][sd]


def _swa_core(qkv, cos_t, sin_t, q_gain, k_gain, sinks, b, s):
    w = SWA_WINDOW
    n = s // w
    m = b * s
    half = SWA_HEAD_DIM // 2
    swapped = lambda g: jnp.concatenate([g[half:], g[:half]])
    qg = jnp.tile(q_gain.astype(F32), SWA_HEADS)[None, :]
    qgs = jnp.tile(swapped(q_gain.astype(F32)), SWA_HEADS)[None, :]
    kg = jnp.tile(k_gain.astype(F32), SWA_KV_HEADS)[None, :]
    kgs = jnp.tile(swapped(k_gain.astype(F32)), SWA_KV_HEADS)[None, :]
    consts = _swa_constants()
    row = lambda bi, ni: bi * n + ni
    const2 = lambda bi, ni: (0, 0)
    return pl.pallas_call(
        _swa_kernel,
        grid=(b, n),
        in_specs=[pl.BlockSpec(memory_space=pltpu.SMEM),
                  pl.BlockSpec((w, SWA_Q), lambda bi, ni: (row(bi, ni), 0)),
                  pl.BlockSpec((w, 2 * SWA_KV), lambda bi, ni: (row(bi, ni), SWA_Q // (2 * SWA_KV))),
                  pl.BlockSpec((w, LANES), lambda bi, ni: (row(bi, ni), 0)),
                  pl.BlockSpec((w, LANES), lambda bi, ni: (row(bi, ni), 0)),
                  pl.BlockSpec((1, SWA_Q), const2), pl.BlockSpec((1, SWA_Q), const2),
                  pl.BlockSpec((1, SWA_KV), const2), pl.BlockSpec((1, SWA_KV), const2),
                  *[pl.BlockSpec(a.shape, const2) for a in consts]],
        out_specs=pl.BlockSpec((w, SWA_Q), lambda bi, ni: (row(bi, ni), 0)),
        out_shape=jax.ShapeDtypeStruct((m, SWA_Q), BF16),
        scratch_shapes=[pltpu.VMEM((2 * SWA_KV_HEADS, w, LANES), BF16),
                        pltpu.VMEM((2 * SWA_KV_HEADS, w, LANES), BF16)],
        compiler_params=_params("parallel", "arbitrary"),
        name="swa_core",
    )(sinks.astype(F32), qkv, qkv, cos_t, sin_t, qg, qgs, kg, kgs, *consts)


GDN_HALO = 8
GDN_STACK = 4
GDN_ROWS = GDN_STACK * GDN_CHUNK
GDN_STEP_CHUNKS = 2


def _gdn_masks():
    i = np.arange(GDN_CHUNK)[:, None]
    j = np.arange(GDN_ROWS)[None, :] % GDN_CHUNK
    r = np.arange(GDN_ROWS)
    same = (r[:, None] // GDN_CHUNK) == (r[None, :] // GDN_CHUNK)
    return jnp.asarray(i >= j, F32), jnp.asarray(i > j, F32), jnp.asarray(same, BF16)


def _gdn_kernel(qkv_ref, z_ref, ba_ref, cw_ref, alog_ref, dtb_ref, og_ref, mt_ref, ms_ref, bd_ref, o_ref,
                ext_ref, q_scr, k_scr, v_scr, s_ref):
    c = GDN_CHUNK
    dh = GDN_HEAD_DIM
    nh = GDN_V_HEADS
    rows = GDN_STEP_CHUNKS * c

    @pl.when(pl.program_id(1) == 0)
    def _():
        s_ref[...] = jnp.zeros_like(s_ref)
        ext_ref[0:GDN_HALO, :] = jnp.zeros((GDN_HALO, GDN_CONV_DIM), F32)

    ext_ref[GDN_HALO:GDN_HALO + rows, :] = qkv_ref[...].astype(F32)

    def conv_block(r0, j):
        tok = slice(r0, r0 + c)
        cs = slice(j * dh, (j + 1) * dh)
        e0 = GDN_HALO + r0
        acc = ext_ref[e0:e0 + c, cs] * cw_ref[GDN_CONV - 1:GDN_CONV, cs]
        for d in range(1, GDN_CONV):
            acc = acc + ext_ref[e0 - d:e0 - d + c, cs] * cw_ref[GDN_CONV - 1 - d:GDN_CONV - d, cs]
        act = acc * _sigmoid(acc)
        if j < 2 * GDN_QK_HEADS:
            nrm = act * lax.rsqrt(jnp.sum(act * act, axis=-1, keepdims=True) + NORM_EPS)
            if j < GDN_QK_HEADS:
                q_scr[tok, cs] = nrm * (dh ** -0.5)
            else:
                k_scr[tok, j * dh - GDN_QK:(j + 1) * dh - GDN_QK] = nrm
        else:
            v_scr[tok, j * dh - 2 * GDN_QK:(j + 1) * dh - 2 * GDN_QK] = act

    ri = lax.broadcasted_iota(jnp.int32, (c, c), 0)
    ci = lax.broadcasted_iota(jnp.int32, (c, c), 1)
    cum = (ri >= ci).astype(F32)
    m_tril = mt_ref[...]
    m_strict = ms_ref[...]
    eye = m_tril - m_strict
    left = lax.broadcasted_iota(jnp.int32, (c, dh), 1) < c
    groups = range(nh // GDN_STACK)
    order = [[GDN_STACK * grp + r for r in (0, 2, 1, 3)] for grp in groups]
    steps = int(math.log2(c)) - 1
    dot = functools.partial(jnp.dot, preferred_element_type=F32)

    def block_diag(packed):
        return jnp.concatenate([packed] * GDN_STACK, axis=0) * bd_ref[...]

    def pack_pairs(x):
        tile = jnp.where(left, x[:c, :], x[c:, :])
        return jnp.concatenate([tile, tile], axis=1)

    def pack_cols(cols, grp):
        h0, h1, h2, h3 = order[grp]
        return jnp.concatenate([jnp.where(left, cols[h0], cols[h1]), jnp.where(left, cols[h2], cols[h3])], axis=1)

    def chunk(r0):
        tok = slice(r0, r0 + c)
        for j in range(GDN_CONV_DIM // dh):
            conv_block(r0, j)
        ba = ba_ref[tok, :]
        beta = pltpu.roll(_sigmoid(ba), nh, 1)
        pre = ba + dtb_ref[...]
        softplus = jnp.maximum(pre, 0.0) + jnp.log1p(jnp.exp(-jnp.abs(pre)))
        g = -jnp.exp(alog_ref[...]) * softplus
        gc = jnp.dot(cum, g, preferred_element_type=F32, precision=lax.Precision.HIGHEST)
        gc_t = gc.T
        g_col = [jnp.broadcast_to(gc[:, nh + hv:nh + hv + 1], (c, dh)) for hv in range(nh)]
        b_col = [jnp.broadcast_to(beta[:, nh + hv:nh + hv + 1], (c, dh)) for hv in range(nh)]
        eg = [jnp.exp(g_col[hv]) for hv in range(nh)]
        g_last = [g_col[hv][c - 1:c, :] for hv in range(nh)]
        q_h = [q_scr[tok, hq * dh:(hq + 1) * dh] for hq in range(GDN_QK_HEADS)]
        k_h = [k_scr[tok, hq * dh:(hq + 1) * dh] for hq in range(GDN_QK_HEADS)]

        low, attn = [], []
        for grp in groups:
            kab = jnp.concatenate([k_h[2 * grp], k_h[2 * grp + 1]], axis=0).astype(BF16)
            qab = jnp.concatenate([q_h[2 * grp], q_h[2 * grp + 1]], axis=0).astype(BF16)
            kk = pack_pairs(lax.dot_general(kab, kab, NT_DIMS, preferred_element_type=F32))
            qk = pack_pairs(lax.dot_general(qab, kab, NT_DIMS, preferred_element_type=F32))
            g_rows = jnp.concatenate([gc_t[nh + hv:nh + hv + 1, :] for hv in order[grp]], axis=1)
            decay = jnp.exp((pack_cols(g_col, grp) - g_rows) * m_tril) * m_tril
            low.append(kk * decay * pack_cols(b_col, grp) * m_strict)
            attn.append((qk * decay).astype(BF16))

        t_inv = [eye - low[grp] for grp in groups]
        pw = [low[grp].astype(BF16) for grp in groups]
        pw = [dot(pw[grp], block_diag(pw[grp])).astype(BF16) for grp in groups]
        for it in range(steps):
            rhs = [block_diag(pw[grp]) for grp in groups]
            if it + 1 < steps:
                both = [dot(jnp.concatenate([pw[grp], t_inv[grp].astype(BF16)], axis=0), rhs[grp]) for grp in groups]
                pw = [both[grp][:c].astype(BF16) for grp in groups]
                t_inv = [t_inv[grp] + both[grp][c:] for grp in groups]
            else:
                t_inv = [t_inv[grp] + dot(t_inv[grp].astype(BF16), rhs[grp]) for grp in groups]

        uw = []
        for grp in groups:
            rhs = jnp.concatenate(
                [jnp.concatenate([v_scr[tok, hv * dh:(hv + 1) * dh] * b_col[hv],
                                  k_h[hv // GDN_REP] * (b_col[hv] * eg[hv])], axis=1) for hv in order[grp]],
                axis=0).astype(BF16)
            uw.append(dot(block_diag(t_inv[grp].astype(BF16)), rhs))

        states = [s_ref[hv] for hv in range(nh)]
        sbs = [st.astype(BF16) for st in states]
        vnb = []
        for grp in groups:
            v_new = jnp.concatenate(
                [uw[grp][r * c:(r + 1) * c, :dh] - dot(uw[grp][r * c:(r + 1) * c, dh:].astype(BF16), sbs[hv])
                 for r, hv in enumerate(order[grp])], axis=0)
            vnb.append(v_new.astype(BF16))
        o_intra = [dot(block_diag(attn[grp]), vnb[grp]) for grp in groups]
        for grp in groups:
            for r, hv in enumerate(order[grp]):
                blk = slice(r * c, (r + 1) * c)
                q_in = (q_h[hv // GDN_REP] * eg[hv]).astype(BF16)
                k_out = (k_h[hv // GDN_REP] * jnp.exp(g_last[hv] - g_col[hv])).astype(BF16)
                o = o_intra[grp][blk] + dot(q_in, sbs[hv])
                s_ref[hv] = states[hv] * jnp.exp(g_last[hv]) + lax.dot_general(
                    k_out, vnb[grp][blk], TN_DIMS, preferred_element_type=F32)
                zz = z_ref[tok, hv * dh:(hv + 1) * dh].astype(F32)
                o_ref[tok, hv * dh:(hv + 1) * dh] = (
                    _rms(o, og_ref[...]) * (zz * _sigmoid(zz))).astype(o_ref.dtype)

    for t in range(GDN_STEP_CHUNKS):
        chunk(t * c)
    ext_ref[0:GDN_HALO, :] = ext_ref[rows:rows + GDN_HALO, :]


def _gdn_core(qkvz, ba, conv_w, alog_row, dtb_row, out_gain, b, s):
    c = GDN_STEP_CHUNKS * GDN_CHUNK
    n = s // c
    m = b * s
    m_tril, m_strict, block_sel = _gdn_masks()
    row = lambda bi, ni: bi * n + ni
    const2 = lambda bi, ni: (0, 0)
    return pl.pallas_call(
        _gdn_kernel,
        grid=(b, n),
        in_specs=[pl.BlockSpec((c, GDN_CONV_DIM), lambda bi, ni: (row(bi, ni), 0)),
                  pl.BlockSpec((c, GDN_V), lambda bi, ni: (row(bi, ni), GDN_CONV_DIM // GDN_V)),
                  pl.BlockSpec((c, LANES), lambda bi, ni: (row(bi, ni), 0)),
                  pl.BlockSpec((GDN_CONV, GDN_CONV_DIM), const2),
                  pl.BlockSpec((1, LANES), const2),
                  pl.BlockSpec((1, LANES), const2),
                  pl.BlockSpec((1, GDN_HEAD_DIM), const2),
                  pl.BlockSpec((GDN_CHUNK, GDN_ROWS), const2),
                  pl.BlockSpec((GDN_CHUNK, GDN_ROWS), const2),
                  pl.BlockSpec((GDN_ROWS, GDN_ROWS), const2)],
        out_specs=pl.BlockSpec((c, GDN_V), lambda bi, ni: (row(bi, ni), 0)),
        out_shape=jax.ShapeDtypeStruct((m, GDN_V), BF16),
        scratch_shapes=[pltpu.VMEM((c + GDN_HALO, GDN_CONV_DIM), F32),
                        pltpu.VMEM((c, GDN_QK), F32),
                        pltpu.VMEM((c, GDN_QK), F32),
                        pltpu.VMEM((c, GDN_V), F32),
                        pltpu.VMEM((GDN_V_HEADS, GDN_HEAD_DIM, GDN_HEAD_DIM), F32)],
        compiler_params=_params("parallel", "arbitrary"),
        name="gdn_core",
    )(qkvz, qkvz, ba, conv_w, alog_row, dtb_row, out_gain, m_tril, m_strict, block_sel)


def kernel(x, positions, norm_mix, norm_mlp, w_up, w_down, ret_w_in, ret_w_out, swa_w_in, swa_q_gain,
           swa_k_gain, swa_sinks, swa_w_out, gdn_w_in, gdn_conv_w, gdn_a_log, gdn_dt_bias, gdn_out_gain,
           gdn_w_out):
    b, s, d = x.shape
    m = b * s
    depth = norm_mix.shape[0]
    x2d = x.reshape(m, d)
    pos_col = positions.reshape(m, 1).astype(F32)
    ret_cos, ret_sin = _rope_tables(pos_col, RET_DK, RET_DK)
    swa_cos, swa_sin = _rope_tables(pos_col, SWA_HEAD_DIM, LANES)

    for i in range(depth):
        mixer, j = i % 3, i // 3
        gain = norm_mix[i][None, :]
        gate = dict()
        if mixer == 0:
            proj = _inproj(x2d, gain, ret_w_in[j].astype(BF16), tn=1024)
            y = _retention_core(proj, ret_cos, ret_sin, b, s)
            w_out = ret_w_out[j]
            gate_cols = (2 * RET_QK + RET_V) // RET_V
            gate = dict(gate=_ret_gate, gate_pieces=RET_HEADS, gate_args=(proj,),
                        gate_specs=lambda tm, idx: (pl.BlockSpec((tm, RET_V), lambda r: (idx(r), gate_cols)),))
        elif mixer == 1:
            proj = _inproj(x2d, gain, swa_w_in[j].astype(BF16), tn=SWA_Q + 2 * SWA_KV)
            y = _swa_core(proj, swa_cos, swa_sin, swa_q_gain[j], swa_k_gain[j], swa_sinks[j], b, s)
            w_out = swa_w_out[j]
        else:
            n_main = GDN_CONV_DIM + GDN_V
            w_main = gdn_w_in[j][:, :n_main].astype(BF16)
            w_gate = jnp.pad(gdn_w_in[j][:, n_main:], ((0, 0), (0, LANES - 2 * GDN_V_HEADS))).astype(BF16)
            proj, ba = _inproj(x2d, gain, w_main, tn=1024, w_aux=w_gate)
            lane_pad = (GDN_V_HEADS, LANES - 2 * GDN_V_HEADS)
            alog_row = jnp.pad(gdn_a_log[j].astype(F32), lane_pad)[None, :]
            dtb_row = jnp.pad(gdn_dt_bias[j].astype(F32), lane_pad)[None, :]
            y = _gdn_core(proj, ba, gdn_conv_w[j].astype(F32), alog_row, dtb_row,
                          gdn_out_gain[j].astype(F32)[None, :], b, s)
            w_out = gdn_w_out[j]
        x2d = _outproj_mlp(x2d, y, w_out.astype(BF16), norm_mlp[i][None, :],
                           w_up[i].astype(BF16), w_down[i].astype(BF16), **gate)
    return x2d.reshape(b, s, d)
```

```python
import functools
import math

import numpy as np
import jax
import jax.numpy as jnp
from jax import lax
from jax.experimental import pallas as pl
from jax.experimental.pallas import tpu as pltpu

F32 = jnp.float32
BF16 = jnp.bfloat16

D_MODEL = 1024
D_FF = 4 * D_MODEL
NORM_EPS = 1e-6
ROPE_THETA = 10000.0

RET_HEADS = 4
RET_DK = 256
RET_DV = 512
RET_QK = RET_HEADS * RET_DK
RET_V = RET_HEADS * RET_DV
RET_CHUNK = 256

SWA_HEADS = 16
SWA_KV_HEADS = 4
SWA_HEAD_DIM = 64
SWA_GROUP = SWA_HEADS // SWA_KV_HEADS
SWA_WINDOW = 128
SWA_Q = SWA_HEADS * SWA_HEAD_DIM
SWA_KV = SWA_KV_HEADS * SWA_HEAD_DIM

GDN_QK_HEADS = 8
GDN_V_HEADS = 16
GDN_HEAD_DIM = 128
GDN_QK = GDN_QK_HEADS * GDN_HEAD_DIM
GDN_V = GDN_V_HEADS * GDN_HEAD_DIM
GDN_CONV_DIM = 2 * GDN_QK + GDN_V
GDN_CONV = 4
GDN_CHUNK = 64
GDN_REP = GDN_V_HEADS // GDN_QK_HEADS

LANES = 128
VMEM_LIMIT = 56 * 1024 * 1024
PROJ_ROWS = 512

NT_DIMS = (((1,), (1,)), ((), ()))
TN_DIMS = (((0,), (0,)), ((), ()))


def _params(*sem):
    return pltpu.CompilerParams(dimension_semantics=sem, vmem_limit_bytes=VMEM_LIMIT)


def _sigmoid(x):
    return 1.0 / (1.0 + jnp.exp(-x))


def _rms(x, gain_row):
    ms = jnp.mean(x * x, axis=-1, keepdims=True)
    return x * lax.rsqrt(ms + NORM_EPS) * gain_row


def _rope_table_kernel(pos_ref, inv_ref, sign_ref, cos_ref, sin_ref):
    ang = pos_ref[...] * inv_ref[...]
    fold = cos_ref.shape[1] // ang.shape[1]
    cos_ref[...] = jnp.concatenate([jnp.cos(ang)] * fold, axis=1)
    sin_ref[...] = jnp.concatenate([jnp.sin(ang)] * fold, axis=1) * sign_ref[...]


def _rope_tables(pos_col, head_dim, width):
    m = pos_col.shape[0]
    half = head_dim // 2
    inv = ROPE_THETA ** (-np.arange(0, head_dim, 2, dtype=np.float32) / head_dim)
    reps = width // head_dim
    inv_row = np.tile(np.concatenate([inv, inv]), reps)[None, :].astype(np.float32)
    sign_row = np.tile(np.concatenate([-np.ones(half), np.ones(half)]), reps)[None, :].astype(np.float32)
    uniq = max(half, LANES) if half % LANES == 0 else width
    inv_row = inv_row[:, :uniq]
    tm = min(m, 1024)
    return pl.pallas_call(
        _rope_table_kernel,
        grid=(m // tm,),
        in_specs=[pl.BlockSpec((tm, 1), lambda i: (i, 0)),
                  pl.BlockSpec((1, uniq), lambda i: (0, 0)),
                  pl.BlockSpec((1, width), lambda i: (0, 0))],
        out_specs=[pl.BlockSpec((tm, width), lambda i: (i, 0)),
                   pl.BlockSpec((tm, width), lambda i: (i, 0))],
        out_shape=[jax.ShapeDtypeStruct((m, width), F32)] * 2,
        compiler_params=_params("parallel"),
        name="rope_tables",
    )(pos_col, jnp.asarray(inv_row), jnp.asarray(sign_row))


def _resident(shape):
    return pl.BlockSpec(shape, lambda i: (0,) * len(shape), pipeline_mode=pl.Buffered(1))


def _inproj_kernel(tn, x_ref, g_ref, w_ref, o_ref):
    h = _rms(x_ref[...], g_ref[...]).astype(BF16)
    for j in range(o_ref.shape[1] // tn):
        cols = slice(j * tn, (j + 1) * tn)
        o_ref[:, cols] = jnp.dot(h, w_ref[:, cols], preferred_element_type=F32).astype(o_ref.dtype)


def _inproj_aux_kernel(tn, x_ref, g_ref, w_ref, wa_ref, o_ref, oa_ref):
    h = _rms(x_ref[...], g_ref[...]).astype(BF16)
    oa_ref[...] = jnp.dot(h, wa_ref[...], preferred_element_type=F32)
    for j in range(o_ref.shape[1] // tn):
        cols = slice(j * tn, (j + 1) * tn)
        o_ref[:, cols] = jnp.dot(h, w_ref[:, cols], preferred_element_type=F32).astype(o_ref.dtype)


def _inproj(x2d, gain, w, tn, w_aux=None):
    m, d = x2d.shape
    n = w.shape[1]
    tm = min(m, PROJ_ROWS)
    x_spec = pl.BlockSpec((tm, d), lambda i: (i, 0))
    o_spec = pl.BlockSpec((tm, n), lambda i: (i, 0))
    if w_aux is None:
        return pl.pallas_call(
            functools.partial(_inproj_kernel, tn), grid=(m // tm,),
            in_specs=[x_spec, _resident((1, d)), _resident((d, n))], out_specs=o_spec,
            out_shape=jax.ShapeDtypeStruct((m, n), BF16),
            compiler_params=_params("parallel"),
            name="inproj",
        )(x2d, gain, w)
    na = w_aux.shape[1]
    return pl.pallas_call(
        functools.partial(_inproj_aux_kernel, tn), grid=(m // tm,),
        in_specs=[x_spec, _resident((1, d)), _resident((d, n)), _resident((d, na))],
        out_specs=[o_spec, pl.BlockSpec((tm, na), lambda i: (i, 0))],
        out_shape=[jax.ShapeDtypeStruct((m, n), BF16), jax.ShapeDtypeStruct((m, na), F32)],
        compiler_params=_params("parallel"),
        name="inproj_aux",
    )(x2d, gain, w, w_aux)


def _ret_gate(p, o_ref, g_ref):
    cols = slice(p * RET_DV, (p + 1) * RET_DV)
    o = o_ref[:, cols].astype(F32)
    mu = jnp.mean(o, axis=-1, keepdims=True)
    oc = o - mu
    var = jnp.mean(oc * oc, axis=-1, keepdims=True)
    g = g_ref[:, cols].astype(F32)
    return cols, (oc * lax.rsqrt(var + NORM_EPS) * (g * _sigmoid(g))).astype(BF16)


def _mlp_chunks(tf, h, wu_ref, wd_ref, o_ref, after_chunk=None):
    for j in range(wu_ref.shape[1] // tf):
        a = jnp.dot(h, wu_ref[:, j * tf:(j + 1) * tf], preferred_element_type=F32)
        a = jnp.maximum(a, 0.0)
        a = (a * a).astype(BF16)
        o_ref[...] += jnp.dot(a, wd_ref[j * tf:(j + 1) * tf, :], preferred_element_type=F32)
        if after_chunk is not None:
            after_chunk(j)


def _outproj_mlp_kernel(tf, x_ref, y_ref, wo_ref, g_ref, wu_ref, wd_ref, o_ref):
    o_ref[...] = x_ref[...] + jnp.dot(y_ref[...], wo_ref[...], preferred_element_type=F32)
    _mlp_chunks(tf, _rms(o_ref[...], g_ref[...]).astype(BF16), wu_ref, wd_ref, o_ref)


def _gated_outproj_mlp_kernel(tf, gate, n_pieces, n_gate, x_ref, *refs):
    gate_refs, (wo_ref, g_ref, wu_ref, wd_ref, o_ref, x1_buf) = refs[:n_gate], refs[n_gate:]

    @pl.when(pl.program_id(0) == 0)
    def _():
        x1_buf[...] = jnp.zeros_like(x1_buf)

    x1 = x1_buf[...]
    o_ref[...] = x1
    every = (wu_ref.shape[1] // tf) // n_pieces

    def outproj_piece(j):
        if (j + 1) % every:
            return
        p = j // every
        cols, y = gate(p, *gate_refs)
        part = jnp.dot(y, wo_ref[cols, :], preferred_element_type=F32)
        x1_buf[...] = (x_ref[...] if p == 0 else x1_buf[...]) + part

    _mlp_chunks(tf, _rms(x1, g_ref[...]).astype(BF16), wu_ref, wd_ref, o_ref, outproj_piece)


def _outproj_mlp(x2d, y, w_out, gain, w_up, w_down, gate=None, gate_pieces=1, gate_args=(), gate_specs=(),
                 tf=512):
    m, d = x2d.shape
    ky = y.shape[1]
    ff = w_up.shape[1]
    tm = min(m, PROJ_ROWS)
    nt = m // tm
    weights = [_resident((ky, d)), _resident((1, d)), _resident((d, ff)), _resident((ff, d))]
    if gate is None:
        return pl.pallas_call(
            functools.partial(_outproj_mlp_kernel, tf),
            grid=(nt,),
            in_specs=[pl.BlockSpec((tm, d), lambda i: (i, 0)), pl.BlockSpec((tm, ky), lambda i: (i, 0)), *weights],
            out_specs=pl.BlockSpec((tm, d), lambda i: (i, 0)),
            out_shape=jax.ShapeDtypeStruct((m, d), F32),
            compiler_params=_params("parallel"),
            name="outproj_mlp",
        )(x2d, y, w_out, gain, w_up, w_down)
    done = lambda i: jnp.maximum(i - 1, 0)
    ahead = lambda i: jnp.minimum(i, nt - 1)
    return pl.pallas_call(
        functools.partial(_gated_outproj_mlp_kernel, tf, gate, gate_pieces, 1 + len(gate_args)),
        grid=(nt + 1,),
        in_specs=[pl.BlockSpec((tm, d), lambda i: (ahead(i), 0)),
                  pl.BlockSpec((tm, ky), lambda i: (ahead(i), 0)),
                  *gate_specs(tm, ahead), *weights],
        out_specs=pl.BlockSpec((tm, d), lambda i: (done(i), 0)),
        out_shape=jax.ShapeDtypeStruct((m, d), F32),
        scratch_shapes=[pltpu.VMEM((tm, d), F32)],
        compiler_params=_params("arbitrary"),
        name="gated_outproj_mlp",
    )(x2d, y, *gate_args, w_out, gain, w_up, w_down)


def _ret_constants():
    c = RET_CHUNK
    idx = np.arange(c, dtype=np.float64)
    log_gamma = np.log1p(-(2.0 ** (-5.0 - np.arange(RET_HEADS, dtype=np.float64))))
    diff = idx[:, None] - idx[None, :]
    scale = RET_DK ** -0.5
    dmat = np.where(diff >= 0, np.exp(log_gamma[:, None, None] * np.maximum(diff, 0.0)), 0.0) * scale
    qd = np.exp(log_gamma[:, None] * (idx + 1.0))
    kd = np.exp(log_gamma[:, None] * (c - 1.0 - idx)) * scale
    qd = np.broadcast_to(qd[:, :, None], (RET_HEADS, c, RET_DK))
    kd = np.broadcast_to(kd[:, :, None], (RET_HEADS, c, RET_DK))
    chunk_decay = [float(v) for v in np.exp(log_gamma * c)]
    return (jnp.asarray(dmat, F32), jnp.asarray(qd, F32), jnp.asarray(kd, F32), chunk_decay)


def _ret_kernel(chunk_decay, q_ref, k_ref, v_ref, cos_ref, sin_ref, dmat_ref, qd_ref, kd_ref,
                o_ref, s_ref):
    @pl.when(pl.program_id(1) == 0)
    def _():
        s_ref[...] = jnp.zeros_like(s_ref)

    cos = cos_ref[...]
    sin = sin_ref[...]
    half = RET_DK // 2

    def rope(t):
        return t * cos + jnp.concatenate([t[:, half:], t[:, :half]], axis=1) * sin

    heads = range(RET_HEADS)
    dot = functools.partial(jnp.dot, preferred_element_type=F32)
    qr = [rope(q_ref[:, h * RET_DK:(h + 1) * RET_DK].astype(F32)) for h in heads]
    kr = [rope(k_ref[:, h * RET_DK:(h + 1) * RET_DK].astype(F32)) for h in heads]
    v = [v_ref[:, h * RET_DV:(h + 1) * RET_DV] for h in heads]
    scores = [lax.dot_general(qr[h].astype(BF16), kr[h].astype(BF16), NT_DIMS, preferred_element_type=F32)
              * dmat_ref[h] for h in heads]
    state = [s_ref[h] for h in heads]
    o_cross = [dot((qr[h] * qd_ref[h]).astype(BF16), state[h].astype(BF16)) for h in heads]
    o = [o_cross[h] + dot(scores[h].astype(BF16), v[h]) for h in heads]
    for h in heads:
        kdec = (kr[h] * kd_ref[h]).astype(BF16)
        s_ref[h] = state[h] * chunk_decay[h] + lax.dot_general(kdec, v[h], TN_DIMS, preferred_element_type=F32)
    for h in heads:
        o_ref[:, h * RET_DV:(h + 1) * RET_DV] = o[h].astype(o_ref.dtype)


def _retention_core(qkvg, cos_t, sin_t, b, s):
    c = RET_CHUNK
    n = s // c
    m = b * s
    dmat, qd, kd, chunk_decay = _ret_constants()
    row = lambda bi, ni: bi * n + ni
    const3 = lambda bi, ni: (0, 0, 0)
    return pl.pallas_call(
        functools.partial(_ret_kernel, chunk_decay),
        grid=(b, n),
        in_specs=[pl.BlockSpec((c, RET_QK), lambda bi, ni: (row(bi, ni), 0)),
                  pl.BlockSpec((c, RET_QK), lambda bi, ni: (row(bi, ni), 1)),
                  pl.BlockSpec((c, RET_V), lambda bi, ni: (row(bi, ni), 1)),
                  pl.BlockSpec((c, RET_DK), lambda bi, ni: (row(bi, ni), 0)),
                  pl.BlockSpec((c, RET_DK), lambda bi, ni: (row(bi, ni), 0)),
                  pl.BlockSpec((RET_HEADS, c, c), const3),
                  pl.BlockSpec((RET_HEADS, c, RET_DK), const3),
                  pl.BlockSpec((RET_HEADS, c, RET_DK), const3)],
        out_specs=pl.BlockSpec((c, RET_V), lambda bi, ni: (row(bi, ni), 0)),
        out_shape=jax.ShapeDtypeStruct((m, RET_V), BF16),
        scratch_shapes=[pltpu.VMEM((RET_HEADS, RET_DK, RET_DV), F32)],
        compiler_params=_params("parallel", "arbitrary"),
        name="retention_core",
    )(qkvg, qkvg, qkvg, cos_t, sin_t, dmat, qd, kd)


def _swa_constants():
    dh, half = SWA_HEAD_DIM, SWA_HEAD_DIM // 2
    i = np.arange(2 * LANES)
    swap = (i[:, None] == ((i[None, :] // dh) * dh + (i[None, :] % dh + half) % dh))
    seg_q = (np.arange(SWA_Q)[:, None] // dh) == np.arange(LANES)[None, :]
    seg_k = (np.arange(SWA_KV)[:, None] // dh) == np.arange(LANES)[None, :]
    bf = lambda a: jnp.asarray(a, BF16)
    return bf(swap), bf(seg_q), bf(seg_k), bf(seg_q.T * (dh ** -0.5)), bf(seg_k.T)


def _swa_kernel(sink_ref, q_ref, kv_ref, cos_ref, sin_ref, qg_ref, qgs_ref, kg_ref, kgs_ref,
                swap_ref, segq_ref, segk_ref, expq_ref, expk_ref, o_ref, k_prev, v_prev):
    w = SWA_WINDOW
    ni = pl.program_id(1)
    dot = functools.partial(jnp.dot, preferred_element_type=F32)

    @pl.when(ni == 0)
    def _():
        k_prev[...] = jnp.zeros_like(k_prev)
        v_prev[...] = jnp.zeros_like(v_prev)

    cos = cos_ref[...]
    sin = sin_ref[...]

    def normed_rope(xb, gain_ref, gain_swapped_ref, seg_ref, exp_ref):
        width = xb.shape[1]
        reps = width // LANES
        xf = xb.astype(F32)
        partner = jnp.concatenate(
            [dot(xb[:, t * 2 * LANES:(t + 1) * 2 * LANES], swap_ref[...]) for t in range(reps // 2)], axis=1)
        rot = (xf * (jnp.concatenate([cos] * reps, axis=1) * gain_ref[...])
               + partner * (jnp.concatenate([sin] * reps, axis=1) * gain_swapped_ref[...]))
        ss = dot((xf * xf).astype(BF16), seg_ref[...])
        inv = lax.rsqrt(ss * (1.0 / SWA_HEAD_DIM) + NORM_EPS)
        inv_hi = inv.astype(BF16)
        inv_lo = (inv - inv_hi.astype(F32)).astype(BF16)
        return rot * (dot(inv_hi, exp_ref[...]) + dot(inv_lo, exp_ref[...]))

    q_n = normed_rope(q_ref[...], qg_ref, qgs_ref, segq_ref, expq_ref).astype(BF16)
    k_n = normed_rope(kv_ref[:, :SWA_KV], kg_ref, kgs_ref, segk_ref, expk_ref)
    v_f = kv_ref[:, SWA_KV:].astype(F32)

    left = lax.broadcasted_iota(jnp.int32, (w, LANES), 1) < SWA_HEAD_DIM

    def placements(x):
        out = []
        for kh in range(SWA_KV_HEADS):
            tile = x[:, (kh // 2) * LANES:(kh // 2 + 1) * LANES]
            moved = pltpu.roll(tile, SWA_HEAD_DIM, 1)
            in_left, in_right = (tile, moved) if kh % 2 == 0 else (moved, tile)
            out.append([jnp.where(left, in_left, 0.0).astype(BF16), jnp.where(left, 0.0, in_right).astype(BF16)])
        return out

    k_cur = placements(k_n)
    v_cur = placements(v_f)

    key = lax.broadcasted_iota(jnp.int32, (2 * w, 2 * w), 0)
    qry = lax.broadcasted_iota(jnp.int32, (2 * w, 2 * w), 1) % w
    rel = qry + w - key
    first_key = jnp.where(ni > 0, 0, w)
    valid = (rel >= 0) & (rel < w) & (key >= first_key)

    pairs = [(kh, sd) for kh in range(SWA_KV_HEADS) for sd in range(2)]
    sc, sink = {}, {}
    for kh, sd in pairs:
        slot = 2 * kh + sd
        k2 = jnp.concatenate([k_prev[slot], k_cur[kh][sd]], axis=0)
        q2 = jnp.concatenate([q_n[:, (2 * kh + t) * LANES:(2 * kh + t + 1) * LANES] for t in range(2)], axis=0)
        sc[kh, sd] = lax.dot_general(k2, q2, NT_DIMS, preferred_element_type=F32)
        heads = [SWA_GROUP * kh + 2 * t + sd for t in range(2)]
        sink[kh, sd] = jnp.concatenate([jnp.full((1, w), sink_ref[h], F32) for h in heads], axis=1)
    sc = {pr: jnp.where(valid, sc[pr], -jnp.inf) for pr in pairs}
    mx = {pr: jnp.maximum(jnp.max(sc[pr], axis=0, keepdims=True), sink[pr]) for pr in pairs}
    p = {pr: jnp.exp(sc[pr] - mx[pr]) for pr in pairs}
    den = {pr: jnp.sum(p[pr], axis=0, keepdims=True) + jnp.exp(sink[pr] - mx[pr]) for pr in pairs}
    p = {pr: (p[pr] * (1.0 / den[pr])).astype(BF16) for pr in pairs}
    for kh in range(SWA_KV_HEADS):
        v2 = jnp.concatenate([v_prev[2 * kh], v_cur[kh][0], v_prev[2 * kh + 1], v_cur[kh][1]], axis=0)
        p2 = jnp.concatenate([p[kh, 0], p[kh, 1]], axis=0)
        o_t = lax.dot_general(v2, p2, TN_DIMS, preferred_element_type=F32)
        for t in range(2):
            o_ref[:, (2 * kh + t) * LANES:(2 * kh + t + 1) * LANES] = o_t[:, t * w:(t + 1) * w].T.astype(o_ref.dtype)
    for kh, sd in pairs:
        k_prev[2 * kh + sd] = k_cur[kh][sd]
        v_prev[2 * kh + sd] = v_cur[kh][sd]


def _swa_core(qkv, cos_t, sin_t, q_gain, k_gain, sinks, b, s):
    w = SWA_WINDOW
    n = s // w
    m = b * s
    half = SWA_HEAD_DIM // 2
    swapped = lambda g: jnp.concatenate([g[half:], g[:half]])
    qg = jnp.tile(q_gain.astype(F32), SWA_HEADS)[None, :]
    qgs = jnp.tile(swapped(q_gain.astype(F32)), SWA_HEADS)[None, :]
    kg = jnp.tile(k_gain.astype(F32), SWA_KV_HEADS)[None, :]
    kgs = jnp.tile(swapped(k_gain.astype(F32)), SWA_KV_HEADS)[None, :]
    consts = _swa_constants()
    row = lambda bi, ni: bi * n + ni
    const2 = lambda bi, ni: (0, 0)
    return pl.pallas_call(
        _swa_kernel,
        grid=(b, n),
        in_specs=[pl.BlockSpec(memory_space=pltpu.SMEM),
                  pl.BlockSpec((w, SWA_Q), lambda bi, ni: (row(bi, ni), 0)),
                  pl.BlockSpec((w, 2 * SWA_KV), lambda bi, ni: (row(bi, ni), SWA_Q // (2 * SWA_KV))),
                  pl.BlockSpec((w, LANES), lambda bi, ni: (row(bi, ni), 0)),
                  pl.BlockSpec((w, LANES), lambda bi, ni: (row(bi, ni), 0)),
                  pl.BlockSpec((1, SWA_Q), const2), pl.BlockSpec((1, SWA_Q), const2),
                  pl.BlockSpec((1, SWA_KV), const2), pl.BlockSpec((1, SWA_KV), const2),
                  *[pl.BlockSpec(a.shape, const2) for a in consts]],
        out_specs=pl.BlockSpec((w, SWA_Q), lambda bi, ni: (row(bi, ni), 0)),
        out_shape=jax.ShapeDtypeStruct((m, SWA_Q), BF16),
        scratch_shapes=[pltpu.VMEM((2 * SWA_KV_HEADS, w, LANES), BF16),
                        pltpu.VMEM((2 * SWA_KV_HEADS, w, LANES), BF16)],
        compiler_params=_params("parallel", "arbitrary"),
        name="swa_core",
    )(sinks.astype(F32), qkv, qkv, cos_t, sin_t, qg, qgs, kg, kgs, *consts)


GDN_STACK = 4
GDN_ROWS = GDN_STACK * GDN_CHUNK
GDN_STEP_CHUNKS = 4


def _gdn_masks():
    i = np.arange(GDN_CHUNK)[:, None]
    j = np.arange(GDN_ROWS)[None, :] % GDN_CHUNK
    r = np.arange(GDN_ROWS)
    same = (r[:, None] // GDN_CHUNK) == (r[None, :] // GDN_CHUNK)
    return jnp.asarray(i >= j, F32), jnp.asarray(i > j, F32), jnp.asarray(same, BF16)


GDN_CONV_ROWS = 256
GDN_CONV_SUB = 64


def _gdn_shift_matrix():
    n = GDN_CONV_SUB
    sh = np.zeros((3 * n, 2 * n), np.float32)
    for d in range(1, GDN_CONV):
        sh[(d - 1) * n + np.arange(n), n + np.arange(n) - d] = 1.0
    return jnp.asarray(sh, BF16)


def _gdn_conv_kernel(prev_ref, cur_ref, cw_ref, shift_ref, o_ref):
    n = GDN_CONV_SUB
    dh = GDN_HEAD_DIM
    wide = 2 * dh
    keep = jnp.where(pl.program_id(1) > 0, 1.0, 0.0).astype(BF16)
    for j in range(GDN_CONV_DIM // wide):
        cs = slice(j * wide, (j + 1) * wide)
        w = [cw_ref[d:d + 1, cs] for d in range(GDN_CONV)]
        for t in range(GDN_CONV_ROWS // n):
            if t == 0:
                window = jnp.concatenate([prev_ref[:, cs] * keep, cur_ref[0:n, cs]], axis=0)
            else:
                window = cur_ref[(t - 1) * n:(t + 1) * n, cs]
            sh = jnp.dot(shift_ref[...], window, preferred_element_type=F32)
            acc = cur_ref[t * n:(t + 1) * n, cs].astype(F32) * w[GDN_CONV - 1]
            for d in range(1, GDN_CONV):
                acc = acc + sh[(d - 1) * n:d * n, :] * w[GDN_CONV - 1 - d]
            act = acc * _sigmoid(acc)
            if j * wide < 2 * GDN_QK:
                scale = dh ** -0.5 if j * wide < GDN_QK else 1.0
                act = jnp.concatenate(
                    [act[:, h * dh:(h + 1) * dh] * (scale * lax.rsqrt(
                        jnp.sum(act[:, h * dh:(h + 1) * dh] ** 2, axis=-1, keepdims=True) + NORM_EPS))
                     for h in range(wide // dh)], axis=1)
            o_ref[t * n:(t + 1) * n, cs] = act.astype(o_ref.dtype)


def _gdn_conv(qkvz, conv_w, b, s):
    r, n = GDN_CONV_ROWS, GDN_CONV_SUB
    nt = s // r
    m = b * s
    return pl.pallas_call(
        _gdn_conv_kernel,
        grid=(b, nt),
        in_specs=[pl.BlockSpec((n, GDN_CONV_DIM), lambda bi, ti: (jnp.maximum((bi * nt + ti) * (r // n) - 1, 0), 0)),
                  pl.BlockSpec((r, GDN_CONV_DIM), lambda bi, ti: (bi * nt + ti, 0)),
                  pl.BlockSpec((GDN_CONV, GDN_CONV_DIM), lambda bi, ti: (0, 0)),
                  pl.BlockSpec((3 * n, 2 * n), lambda bi, ti: (0, 0))],
        out_specs=pl.BlockSpec((r, GDN_CONV_DIM), lambda bi, ti: (bi * nt + ti, 0)),
        out_shape=jax.ShapeDtypeStruct((m, GDN_CONV_DIM), BF16),
        compiler_params=_params("parallel", "parallel"),
        name="gdn_conv",
    )(qkvz, qkvz, conv_w, _gdn_shift_matrix())


def _gdn_kernel(qkv_ref, z_ref, ba_ref, alog_ref, dtb_ref, og_ref, mt_ref, ms_ref, bd_ref, o_ref, s_ref):
    c = GDN_CHUNK
    dh = GDN_HEAD_DIM
    nh = GDN_V_HEADS

    @pl.when(pl.program_id(1) == 0)
    def _():
        s_ref[...] = jnp.zeros_like(s_ref)

    ri = lax.broadcasted_iota(jnp.int32, (c, c), 0)
    ci = lax.broadcasted_iota(jnp.int32, (c, c), 1)
    cum = (ri >= ci).astype(F32)
    m_tril = mt_ref[...]
    m_strict = ms_ref[...]
    eye = m_tril - m_strict
    left = lax.broadcasted_iota(jnp.int32, (c, dh), 1) < c
    groups = range(nh // GDN_STACK)
    order = [[GDN_STACK * grp + r for r in (0, 2, 1, 3)] for grp in groups]
    steps = int(math.log2(c)) - 1
    dot = functools.partial(jnp.dot, preferred_element_type=F32)

    def block_diag(packed):
        return jnp.concatenate([packed] * GDN_STACK, axis=0) * bd_ref[...]

    def pack_pairs(x):
        tile = jnp.where(left, x[:c, :], x[c:, :])
        return jnp.concatenate([tile, tile], axis=1)

    def pack_cols(cols, grp):
        h0, h1, h2, h3 = order[grp]
        return jnp.concatenate([jnp.where(left, cols[h0], cols[h1]), jnp.where(left, cols[h2], cols[h3])], axis=1)

    def prepare(r0):
        tok = slice(r0, r0 + c)
        ba = ba_ref[tok, :]
        beta = pltpu.roll(_sigmoid(ba), nh, 1)
        pre = ba + dtb_ref[...]
        softplus = jnp.maximum(pre, 0.0) + jnp.log1p(jnp.exp(-jnp.abs(pre)))
        g = -jnp.exp(alog_ref[...]) * softplus
        gc = jnp.dot(cum, g, preferred_element_type=F32, precision=lax.Precision.HIGHEST)
        gc_t = gc.T
        g_col = [jnp.broadcast_to(gc[:, nh + hv:nh + hv + 1], (c, dh)) for hv in range(nh)]
        b_col = [jnp.broadcast_to(beta[:, nh + hv:nh + hv + 1], (c, dh)) for hv in range(nh)]
        eg = [jnp.exp(g_col[hv]) for hv in range(nh)]
        g_last = [g_col[hv][c - 1:c, :] for hv in range(nh)]
        qb = [qkv_ref[tok, hq * dh:(hq + 1) * dh] for hq in range(GDN_QK_HEADS)]
        kb = [qkv_ref[tok, GDN_QK + hq * dh:GDN_QK + (hq + 1) * dh] for hq in range(GDN_QK_HEADS)]
        q_h = [t.astype(F32) for t in qb]
        k_h = [t.astype(F32) for t in kb]

        low, attn = [], []
        for grp in groups:
            kab = jnp.concatenate([kb[2 * grp], kb[2 * grp + 1]], axis=0)
            qab = jnp.concatenate([qb[2 * grp], qb[2 * grp + 1]], axis=0)
            kk = pack_pairs(lax.dot_general(kab, kab, NT_DIMS, preferred_element_type=F32))
            qk = pack_pairs(lax.dot_general(qab, kab, NT_DIMS, preferred_element_type=F32))
            g_rows = jnp.concatenate([gc_t[nh + hv:nh + hv + 1, :] for hv in order[grp]], axis=1)
            decay = jnp.exp((pack_cols(g_col, grp) - g_rows) * m_tril) * m_tril
            low.append(kk * decay * pack_cols(b_col, grp) * m_strict)
            attn.append((qk * decay).astype(BF16))
        return dict(g_col=g_col, b_col=b_col, eg=eg, g_last=g_last, q_h=q_h, k_h=k_h, low=low, attn=attn)

    def inverse(low):
        idx = range(len(low))
        t_inv = [eye - low[i] for i in idx]
        pw = [low[i].astype(BF16) for i in idx]
        pw = [dot(pw[i], block_diag(pw[i])).astype(BF16) for i in idx]
        for it in range(steps):
            rhs = [block_diag(pw[i]) for i in idx]
            if it + 1 < steps:
                both = [dot(jnp.concatenate([pw[i], t_inv[i].astype(BF16)], axis=0), rhs[i]) for i in idx]
                pw = [both[i][:c].astype(BF16) for i in idx]
                t_inv = [t_inv[i] + both[i][c:] for i in idx]
            else:
                t_inv = [t_inv[i] + dot(t_inv[i].astype(BF16), rhs[i]) for i in idx]
        return t_inv

    def recur(r0, p, t_inv):
        tok = slice(r0, r0 + c)
        g_col, b_col, eg, g_last, q_h, k_h = p["g_col"], p["b_col"], p["eg"], p["g_last"], p["q_h"], p["k_h"]
        attn = p["attn"]
        uw = []
        for grp in groups:
            rhs = jnp.concatenate(
                [jnp.concatenate([qkv_ref[tok, 2 * GDN_QK + hv * dh:2 * GDN_QK + (hv + 1) * dh].astype(F32) * b_col[hv],
                                  k_h[hv // GDN_REP] * (b_col[hv] * eg[hv])], axis=1) for hv in order[grp]],
                axis=0).astype(BF16)
            uw.append(dot(block_diag(t_inv[grp].astype(BF16)), rhs))

        states = [s_ref[hv] for hv in range(nh)]
        sbs = [st.astype(BF16) for st in states]
        vnb = []
        for grp in groups:
            v_new = jnp.concatenate(
                [uw[grp][r * c:(r + 1) * c, :dh] - dot(uw[grp][r * c:(r + 1) * c, dh:].astype(BF16), sbs[hv])
                 for r, hv in enumerate(order[grp])], axis=0)
            vnb.append(v_new.astype(BF16))
        o_intra = [dot(block_diag(attn[grp]), vnb[grp]) for grp in groups]
        for grp in groups:
            for r, hv in enumerate(order[grp]):
                blk = slice(r * c, (r + 1) * c)
                q_in = (q_h[hv // GDN_REP] * eg[hv]).astype(BF16)
                k_out = (k_h[hv // GDN_REP] * jnp.exp(g_last[hv] - g_col[hv])).astype(BF16)
                o = o_intra[grp][blk] + dot(q_in, sbs[hv])
                s_ref[hv] = states[hv] * jnp.exp(g_last[hv]) + lax.dot_general(
                    k_out, vnb[grp][blk], TN_DIMS, preferred_element_type=F32)
                zz = z_ref[tok, hv * dh:(hv + 1) * dh].astype(F32)
                o_ref[tok, hv * dh:(hv + 1) * dh] = (
                    _rms(o, og_ref[...]) * (zz * _sigmoid(zz))).astype(o_ref.dtype)

    prepared = [prepare(t * c) for t in range(GDN_STEP_CHUNKS)]
    t_inv = inverse([low for p in prepared for low in p["low"]])
    for t, p in enumerate(prepared):
        recur(t * c, p, t_inv[t * len(groups):(t + 1) * len(groups)])


def _gdn_core(qkv, qkvz, ba, alog_row, dtb_row, out_gain, b, s):
    c = GDN_STEP_CHUNKS * GDN_CHUNK
    n = s // c
    m = b * s
    m_tril, m_strict, block_sel = _gdn_masks()
    row = lambda bi, ni: bi * n + ni
    const2 = lambda bi, ni: (0, 0)
    return pl.pallas_call(
        _gdn_kernel,
        grid=(b, n),
        in_specs=[pl.BlockSpec((c, GDN_CONV_DIM), lambda bi, ni: (row(bi, ni), 0)),
                  pl.BlockSpec((c, GDN_V), lambda bi, ni: (row(bi, ni), GDN_CONV_DIM // GDN_V)),
                  pl.BlockSpec((c, LANES), lambda bi, ni: (row(bi, ni), 0)),
                  pl.BlockSpec((1, LANES), const2),
                  pl.BlockSpec((1, LANES), const2),
                  pl.BlockSpec((1, GDN_HEAD_DIM), const2),
                  pl.BlockSpec((GDN_CHUNK, GDN_ROWS), const2),
                  pl.BlockSpec((GDN_CHUNK, GDN_ROWS), const2),
                  pl.BlockSpec((GDN_ROWS, GDN_ROWS), const2)],
        out_specs=pl.BlockSpec((c, GDN_V), lambda bi, ni: (row(bi, ni), 0)),
        out_shape=jax.ShapeDtypeStruct((m, GDN_V), BF16),
        scratch_shapes=[pltpu.VMEM((GDN_V_HEADS, GDN_HEAD_DIM, GDN_HEAD_DIM), F32)],
        compiler_params=_params("parallel", "arbitrary"),
        name="gdn_core",
    )(qkv, qkvz, ba, alog_row, dtb_row, out_gain, m_tril, m_strict, block_sel)


def kernel(x, positions, norm_mix, norm_mlp, w_up, w_down, ret_w_in, ret_w_out, swa_w_in, swa_q_gain,
           swa_k_gain, swa_sinks, swa_w_out, gdn_w_in, gdn_conv_w, gdn_a_log, gdn_dt_bias, gdn_out_gain,
           gdn_w_out):
    b, s, d = x.shape
    m = b * s
    depth = norm_mix.shape[0]
    x2d = x.reshape(m, d)
    pos_col = positions.reshape(m, 1).astype(F32)
    ret_cos, ret_sin = _rope_tables(pos_col, RET_DK, RET_DK)
    swa_cos, swa_sin = _rope_tables(pos_col, SWA_HEAD_DIM, LANES)

    for i in range(depth):
        mixer, j = i % 3, i // 3
        gain = norm_mix[i][None, :]
        gate = dict()
        if mixer == 0:
            proj = _inproj(x2d, gain, ret_w_in[j].astype(BF16), tn=1024)
            y = _retention_core(proj, ret_cos, ret_sin, b, s)
            w_out = ret_w_out[j]
            gate_cols = (2 * RET_QK + RET_V) // RET_V
            gate = dict(gate=_ret_gate, gate_pieces=RET_HEADS, gate_args=(proj,),
                        gate_specs=lambda tm, idx: (pl.BlockSpec((tm, RET_V), lambda r: (idx(r), gate_cols)),))
        elif mixer == 1:
            proj = _inproj(x2d, gain, swa_w_in[j].astype(BF16), tn=SWA_Q + 2 * SWA_KV)
            y = _swa_core(proj, swa_cos, swa_sin, swa_q_gain[j], swa_k_gain[j], swa_sinks[j], b, s)
            w_out = swa_w_out[j]
        else:
            n_main = GDN_CONV_DIM + GDN_V
            w_main = gdn_w_in[j][:, :n_main].astype(BF16)
            w_gate = jnp.pad(gdn_w_in[j][:, n_main:], ((0, 0), (0, LANES - 2 * GDN_V_HEADS))).astype(BF16)
            proj, ba = _inproj(x2d, gain, w_main, tn=1024, w_aux=w_gate)
            lane_pad = (GDN_V_HEADS, LANES - 2 * GDN_V_HEADS)
            alog_row = jnp.pad(gdn_a_log[j].astype(F32), lane_pad)[None, :]
            dtb_row = jnp.pad(gdn_dt_bias[j].astype(F32), lane_pad)[None, :]
            qkv_act = _gdn_conv(proj, gdn_conv_w[j].astype(F32), b, s)
            y = _gdn_core(qkv_act, proj, ba, alog_row, dtb_row, gdn_out_gain[j].astype(F32)[None, :], b, s)
            w_out = gdn_w_out[j]
        x2d = _outproj_mlp(x2d, y, w_out.astype(BF16), norm_mlp[i][None, :],
                           w_up[i].astype(BF16), w_down[i].astype(BF16), **gate)
    return x2d.reshape(b, s, d)
```

```python
import functools
import math

import numpy as np
import jax
import jax.numpy as jnp
from jax import lax
from jax.experimental import pallas as pl
from jax.experimental.pallas import tpu as pltpu

F32 = jnp.float32
BF16 = jnp.bfloat16

D_MODEL = 1024
D_FF = 4 * D_MODEL
NORM_EPS = 1e-6
ROPE_THETA = 10000.0

RET_HEADS = 4
RET_DK = 256
RET_DV = 512
RET_QK = RET_HEADS * RET_DK
RET_V = RET_HEADS * RET_DV
RET_CHUNK = 256

SWA_HEADS = 16
SWA_KV_HEADS = 4
SWA_HEAD_DIM = 64
SWA_GROUP = SWA_HEADS // SWA_KV_HEADS
SWA_WINDOW = 128
SWA_STEP_BLOCKS = 4
SWA_Q = SWA_HEADS * SWA_HEAD_DIM
SWA_KV = SWA_KV_HEADS * SWA_HEAD_DIM

GDN_QK_HEADS = 8
GDN_V_HEADS = 16
GDN_HEAD_DIM = 128
GDN_QK = GDN_QK_HEADS * GDN_HEAD_DIM
GDN_V = GDN_V_HEADS * GDN_HEAD_DIM
GDN_CONV_DIM = 2 * GDN_QK + GDN_V
GDN_CONV = 4
GDN_CHUNK = 64
GDN_REP = GDN_V_HEADS // GDN_QK_HEADS

LANES = 128
VMEM_LIMIT = 56 * 1024 * 1024
PROJ_ROWS = 512

LOG2_E = math.log2(math.e)

NT_DIMS = (((1,), (1,)), ((), ()))
TN_DIMS = (((0,), (0,)), ((), ()))


def _params(*sem):
    return pltpu.CompilerParams(dimension_semantics=sem, vmem_limit_bytes=VMEM_LIMIT)


def _sigmoid(x):
    return 1.0 / (1.0 + jnp.exp2(x * -LOG2_E))


def _rms(x, gain_row):
    ms = jnp.mean(x * x, axis=-1, keepdims=True)
    return x * lax.rsqrt(ms + NORM_EPS) * gain_row


def _rope_table_kernel(pos_ref, inv_ref, sign_ref, cos_ref, sin_ref):
    ang = pos_ref[...] * inv_ref[...]
    fold = cos_ref.shape[1] // ang.shape[1]
    cos_ref[...] = jnp.concatenate([jnp.cos(ang)] * fold, axis=1)
    sin_ref[...] = jnp.concatenate([jnp.sin(ang)] * fold, axis=1) * sign_ref[...]


def _rope_tables(pos_col, head_dim, width):
    m = pos_col.shape[0]
    half = head_dim // 2
    inv = ROPE_THETA ** (-np.arange(0, head_dim, 2, dtype=np.float32) / head_dim)
    reps = width // head_dim
    inv_row = np.tile(np.concatenate([inv, inv]), reps)[None, :].astype(np.float32)
    sign_row = np.tile(np.concatenate([-np.ones(half), np.ones(half)]), reps)[None, :].astype(np.float32)
    uniq = max(half, LANES) if half % LANES == 0 else width
    inv_row = inv_row[:, :uniq]
    tm = min(m, 1024)
    return pl.pallas_call(
        _rope_table_kernel,
        grid=(m // tm,),
        in_specs=[pl.BlockSpec((tm, 1), lambda i: (i, 0)),
                  pl.BlockSpec((1, uniq), lambda i: (0, 0)),
                  pl.BlockSpec((1, width), lambda i: (0, 0))],
        out_specs=[pl.BlockSpec((tm, width), lambda i: (i, 0)),
                   pl.BlockSpec((tm, width), lambda i: (i, 0))],
        out_shape=[jax.ShapeDtypeStruct((m, width), F32)] * 2,
        compiler_params=_params("parallel"),
        name="rope_tables",
    )(pos_col, jnp.asarray(inv_row), jnp.asarray(sign_row))


def _resident(shape):
    return pl.BlockSpec(shape, lambda i: (0,) * len(shape), pipeline_mode=pl.Buffered(1))


def _inproj_kernel(tn, x_ref, g_ref, w_ref, o_ref):
    h = _rms(x_ref[...], g_ref[...]).astype(BF16)
    for j in range(o_ref.shape[1] // tn):
        cols = slice(j * tn, (j + 1) * tn)
        o_ref[:, cols] = jnp.dot(h, w_ref[:, cols], preferred_element_type=F32).astype(o_ref.dtype)


def _inproj_aux_kernel(tn, x_ref, g_ref, w_ref, wa_ref, o_ref, oa_ref):
    h = _rms(x_ref[...], g_ref[...]).astype(BF16)
    oa_ref[...] = jnp.dot(h, wa_ref[...], preferred_element_type=F32)
    for j in range(o_ref.shape[1] // tn):
        cols = slice(j * tn, (j + 1) * tn)
        o_ref[:, cols] = jnp.dot(h, w_ref[:, cols], preferred_element_type=F32).astype(o_ref.dtype)


def _inproj(x2d, gain, w, tn, w_aux=None):
    m, d = x2d.shape
    n = w.shape[1]
    tm = min(m, PROJ_ROWS)
    x_spec = pl.BlockSpec((tm, d), lambda i: (i, 0))
    o_spec = pl.BlockSpec((tm, n), lambda i: (i, 0))
    if w_aux is None:
        return pl.pallas_call(
            functools.partial(_inproj_kernel, tn), grid=(m // tm,),
            in_specs=[x_spec, _resident((1, d)), _resident((d, n))], out_specs=o_spec,
            out_shape=jax.ShapeDtypeStruct((m, n), BF16),
            compiler_params=_params("parallel"),
            name="inproj",
        )(x2d, gain, w)
    na = w_aux.shape[1]
    return pl.pallas_call(
        functools.partial(_inproj_aux_kernel, tn), grid=(m // tm,),
        in_specs=[x_spec, _resident((1, d)), _resident((d, n)), _resident((d, na))],
        out_specs=[o_spec, pl.BlockSpec((tm, na), lambda i: (i, 0))],
        out_shape=[jax.ShapeDtypeStruct((m, n), BF16), jax.ShapeDtypeStruct((m, na), F32)],
        compiler_params=_params("parallel"),
        name="inproj_aux",
    )(x2d, gain, w, w_aux)


def _ret_gate(p, o_ref, g_ref):
    cols = slice(p * RET_DV, (p + 1) * RET_DV)
    o = o_ref[:, cols].astype(F32)
    mu = jnp.mean(o, axis=-1, keepdims=True)
    oc = o - mu
    var = jnp.mean(oc * oc, axis=-1, keepdims=True)
    g = g_ref[:, cols].astype(F32)
    return cols, (oc * lax.rsqrt(var + NORM_EPS) * (g * _sigmoid(g))).astype(BF16)


def _mlp_chunks(tf, h, wu_ref, wd_ref, o_ref, after_chunk=None):
    for j in range(wu_ref.shape[1] // tf):
        a = jnp.dot(h, wu_ref[:, j * tf:(j + 1) * tf], preferred_element_type=F32)
        a = jnp.maximum(a, 0.0)
        a = (a * a).astype(BF16)
        o_ref[...] += jnp.dot(a, wd_ref[j * tf:(j + 1) * tf, :], preferred_element_type=F32)
        if after_chunk is not None:
            after_chunk(j)


def _outproj_mlp_kernel(tf, x_ref, y_ref, wo_ref, g_ref, wu_ref, wd_ref, o_ref):
    o_ref[...] = x_ref[...] + jnp.dot(y_ref[...], wo_ref[...], preferred_element_type=F32)
    _mlp_chunks(tf, _rms(o_ref[...], g_ref[...]).astype(BF16), wu_ref, wd_ref, o_ref)


def _gated_outproj_mlp_kernel(tf, gate, n_pieces, n_gate, x_ref, *refs):
    gate_refs, (wo_ref, g_ref, wu_ref, wd_ref, o_ref, x1_buf) = refs[:n_gate], refs[n_gate:]

    @pl.when(pl.program_id(0) == 0)
    def _():
        x1_buf[...] = jnp.zeros_like(x1_buf)

    x1 = x1_buf[...]
    o_ref[...] = x1
    every = (wu_ref.shape[1] // tf) // n_pieces

    def outproj_piece(j):
        if (j + 1) % every:
            return
        p = j // every
        cols, y = gate(p, *gate_refs)
        part = jnp.dot(y, wo_ref[cols, :], preferred_element_type=F32)
        x1_buf[...] = (x_ref[...] if p == 0 else x1_buf[...]) + part

    _mlp_chunks(tf, _rms(x1, g_ref[...]).astype(BF16), wu_ref, wd_ref, o_ref, outproj_piece)


def _outproj_mlp(x2d, y, w_out, gain, w_up, w_down, gate=None, gate_pieces=1, gate_args=(), gate_specs=(),
                 tf=512):
    m, d = x2d.shape
    ky = y.shape[1]
    ff = w_up.shape[1]
    tm = min(m, PROJ_ROWS)
    nt = m // tm
    weights = [_resident((ky, d)), _resident((1, d)), _resident((d, ff)), _resident((ff, d))]
    if gate is None:
        return pl.pallas_call(
            functools.partial(_outproj_mlp_kernel, tf),
            grid=(nt,),
            in_specs=[pl.BlockSpec((tm, d), lambda i: (i, 0)), pl.BlockSpec((tm, ky), lambda i: (i, 0)), *weights],
            out_specs=pl.BlockSpec((tm, d), lambda i: (i, 0)),
            out_shape=jax.ShapeDtypeStruct((m, d), F32),
            compiler_params=_params("parallel"),
            name="outproj_mlp",
        )(x2d, y, w_out, gain, w_up, w_down)
    done = lambda i: jnp.maximum(i - 1, 0)
    ahead = lambda i: jnp.minimum(i, nt - 1)
    return pl.pallas_call(
        functools.partial(_gated_outproj_mlp_kernel, tf, gate, gate_pieces, 1 + len(gate_args)),
        grid=(nt + 1,),
        in_specs=[pl.BlockSpec((tm, d), lambda i: (ahead(i), 0)),
                  pl.BlockSpec((tm, ky), lambda i: (ahead(i), 0)),
                  *gate_specs(tm, ahead), *weights],
        out_specs=pl.BlockSpec((tm, d), lambda i: (done(i), 0)),
        out_shape=jax.ShapeDtypeStruct((m, d), F32),
        scratch_shapes=[pltpu.VMEM((tm, d), F32)],
        compiler_params=_params("arbitrary"),
        name="gated_outproj_mlp",
    )(x2d, y, *gate_args, w_out, gain, w_up, w_down)


def _ret_constants():
    c = RET_CHUNK
    idx = np.arange(c, dtype=np.float64)
    log_gamma = np.log1p(-(2.0 ** (-5.0 - np.arange(RET_HEADS, dtype=np.float64))))
    diff = idx[:, None] - idx[None, :]
    scale = RET_DK ** -0.5
    dmat = np.where(diff >= 0, np.exp(log_gamma[:, None, None] * np.maximum(diff, 0.0)), 0.0) * scale
    qd = np.exp(log_gamma[:, None] * (idx + 1.0))
    kd = np.exp(log_gamma[:, None] * (c - 1.0 - idx)) * scale
    qd = np.broadcast_to(qd[:, :, None], (RET_HEADS, c, RET_DK))
    kd = np.broadcast_to(kd[:, :, None], (RET_HEADS, c, RET_DK))
    chunk_decay = [float(v) for v in np.exp(log_gamma * c)]
    return (jnp.asarray(dmat, F32), jnp.asarray(qd, F32), jnp.asarray(kd, F32), chunk_decay)


def _ret_kernel(chunk_decay, q_ref, k_ref, v_ref, cos_ref, sin_ref, dmat_ref, qd_ref, kd_ref,
                o_ref, s_ref):
    @pl.when(pl.program_id(1) == 0)
    def _():
        s_ref[...] = jnp.zeros_like(s_ref)

    cos = cos_ref[...]
    sin = sin_ref[...]
    half = RET_DK // 2

    def rope(t):
        return t * cos + jnp.concatenate([t[:, half:], t[:, :half]], axis=1) * sin

    heads = range(RET_HEADS)
    dot = functools.partial(jnp.dot, preferred_element_type=F32)
    qr = [rope(q_ref[:, h * RET_DK:(h + 1) * RET_DK].astype(F32)) for h in heads]
    kr = [rope(k_ref[:, h * RET_DK:(h + 1) * RET_DK].astype(F32)) for h in heads]
    v = [v_ref[:, h * RET_DV:(h + 1) * RET_DV] for h in heads]
    scores = [lax.dot_general(qr[h].astype(BF16), kr[h].astype(BF16), NT_DIMS, preferred_element_type=F32)
              * dmat_ref[h] for h in heads]
    state = [s_ref[h] for h in heads]
    o_cross = [dot((qr[h] * qd_ref[h]).astype(BF16), state[h].astype(BF16)) for h in heads]
    o = [o_cross[h] + dot(scores[h].astype(BF16), v[h]) for h in heads]
    for h in heads:
        kdec = (kr[h] * kd_ref[h]).astype(BF16)
        s_ref[h] = state[h] * chunk_decay[h] + lax.dot_general(kdec, v[h], TN_DIMS, preferred_element_type=F32)
    for h in heads:
        o_ref[:, h * RET_DV:(h + 1) * RET_DV] = o[h].astype(o_ref.dtype)


def _retention_core(qkvg, cos_t, sin_t, b, s):
    c = RET_CHUNK
    n = s // c
    m = b * s
    dmat, qd, kd, chunk_decay = _ret_constants()
    row = lambda bi, ni: bi * n + ni
    const3 = lambda bi, ni: (0, 0, 0)
    return pl.pallas_call(
        functools.partial(_ret_kernel, chunk_decay),
        grid=(b, n),
        in_specs=[pl.BlockSpec((c, RET_QK), lambda bi, ni: (row(bi, ni), 0)),
                  pl.BlockSpec((c, RET_QK), lambda bi, ni: (row(bi, ni), 1)),
                  pl.BlockSpec((c, RET_V), lambda bi, ni: (row(bi, ni), 1)),
                  pl.BlockSpec((c, RET_DK), lambda bi, ni: (row(bi, ni), 0)),
                  pl.BlockSpec((c, RET_DK), lambda bi, ni: (row(bi, ni), 0)),
                  pl.BlockSpec((RET_HEADS, c, c), const3),
                  pl.BlockSpec((RET_HEADS, c, RET_DK), const3),
                  pl.BlockSpec((RET_HEADS, c, RET_DK), const3)],
        out_specs=pl.BlockSpec((c, RET_V), lambda bi, ni: (row(bi, ni), 0)),
        out_shape=jax.ShapeDtypeStruct((m, RET_V), BF16),
        scratch_shapes=[pltpu.VMEM((RET_HEADS, RET_DK, RET_DV), F32)],
        compiler_params=_params("parallel", "arbitrary"),
        name="retention_core",
    )(qkvg, qkvg, qkvg, cos_t, sin_t, dmat, qd, kd)


def _swa_constants():
    dh, half = SWA_HEAD_DIM, SWA_HEAD_DIM // 2
    i = np.arange(2 * LANES)
    swap = (i[:, None] == ((i[None, :] // dh) * dh + (i[None, :] % dh + half) % dh))
    seg_q = (np.arange(SWA_Q)[:, None] // dh) == np.arange(LANES)[None, :]
    seg_k = (np.arange(SWA_KV)[:, None] // dh) == np.arange(LANES)[None, :]
    bf = lambda a: jnp.asarray(a, BF16)
    return bf(swap), bf(seg_q), bf(seg_k), bf(seg_q.T * (dh ** -0.5)), bf(seg_k.T)


def _swa_kernel(sink_ref, q_ref, kv_ref, cos_ref, sin_ref, qg_ref, qgs_ref, kg_ref, kgs_ref,
                swap_ref, segq_ref, segk_ref, expq_ref, expk_ref, o_ref, k_prev, v_prev):
    w = SWA_WINDOW
    ni = pl.program_id(1)
    dot = functools.partial(jnp.dot, preferred_element_type=F32)

    @pl.when(ni == 0)
    def _():
        k_prev[...] = jnp.zeros_like(k_prev)
        v_prev[...] = jnp.zeros_like(v_prev)

    cos = cos_ref[...]
    sin = sin_ref[...]

    def normed_rope(xb, gain_ref, gain_swapped_ref, seg_ref, exp_ref):
        width = xb.shape[1]
        reps = width // LANES
        xf = xb.astype(F32)
        partner = jnp.concatenate(
            [dot(xb[:, t * 2 * LANES:(t + 1) * 2 * LANES], swap_ref[...]) for t in range(reps // 2)], axis=1)
        rot = (xf * (jnp.concatenate([cos] * reps, axis=1) * gain_ref[...])
               + partner * (jnp.concatenate([sin] * reps, axis=1) * gain_swapped_ref[...]))
        ss = dot((xf * xf).astype(BF16), seg_ref[...])
        inv = lax.rsqrt(ss * (1.0 / SWA_HEAD_DIM) + NORM_EPS)
        inv_hi = inv.astype(BF16)
        inv_lo = (inv - inv_hi.astype(F32)).astype(BF16)
        return rot * (dot(inv_hi, exp_ref[...]) + dot(inv_lo, exp_ref[...]))

    q_n = normed_rope(q_ref[...], qg_ref, qgs_ref, segq_ref, expq_ref).astype(BF16)
    k_n = normed_rope(kv_ref[:, :SWA_KV], kg_ref, kgs_ref, segk_ref, expk_ref)
    v_f = kv_ref[:, SWA_KV:].astype(F32)

    left = lax.broadcasted_iota(jnp.int32, (SWA_STEP_BLOCKS * w, LANES), 1) < SWA_HEAD_DIM

    def placements(x):
        out = []
        for kh in range(SWA_KV_HEADS):
            tile = x[:, (kh // 2) * LANES:(kh // 2 + 1) * LANES]
            moved = pltpu.roll(tile, SWA_HEAD_DIM, 1)
            in_left, in_right = (tile, moved) if kh % 2 == 0 else (moved, tile)
            out.append([jnp.where(left, in_left, 0.0).astype(BF16), jnp.where(left, 0.0, in_right).astype(BF16)])
        return out

    k_cur = placements(k_n)
    v_cur = placements(v_f)

    key = lax.broadcasted_iota(jnp.int32, (2 * w, 2 * w), 0)
    qry = lax.broadcasted_iota(jnp.int32, (2 * w, 2 * w), 1) % w
    rel = qry + w - key
    band = (rel >= 0) & (rel < w)
    first_key = jnp.where(ni > 0, 0, w)
    valid = [band & (key >= first_key)] + [band] * (SWA_STEP_BLOCKS - 1)

    def with_prev(cur, prev_ref, slot, blk):
        if blk == 0:
            return jnp.concatenate([prev_ref[slot], cur[:w]], axis=0)
        return cur[(blk - 1) * w:(blk + 1) * w]

    pairs = [(blk, kh, sd) for blk in range(SWA_STEP_BLOCKS) for kh in range(SWA_KV_HEADS) for sd in range(2)]
    sc, sink = {}, {}
    for blk, kh, sd in pairs:
        k2 = with_prev(k_cur[kh][sd], k_prev, 2 * kh + sd, blk)
        q2 = jnp.concatenate([q_n[blk * w:(blk + 1) * w, (2 * kh + t) * LANES:(2 * kh + t + 1) * LANES]
                              for t in range(2)], axis=0)
        sc[blk, kh, sd] = lax.dot_general(k2, q2, NT_DIMS, preferred_element_type=F32)
        heads = [SWA_GROUP * kh + 2 * t + sd for t in range(2)]
        sink[blk, kh, sd] = jnp.concatenate([jnp.full((1, w), sink_ref[h], F32) for h in heads], axis=1)
    sc = {pr: jnp.where(valid[pr[0]], sc[pr], -jnp.inf) for pr in pairs}
    mx = {pr: jnp.maximum(jnp.max(sc[pr], axis=0, keepdims=True), sink[pr]) for pr in pairs}
    p = {pr: jnp.exp(sc[pr] - mx[pr]) for pr in pairs}
    den = {pr: jnp.sum(p[pr], axis=0, keepdims=True) + jnp.exp(sink[pr] - mx[pr]) for pr in pairs}
    p = {pr: (p[pr] * (1.0 / den[pr])).astype(BF16) for pr in pairs}
    for blk in range(SWA_STEP_BLOCKS):
        for kh in range(SWA_KV_HEADS):
            v2 = jnp.concatenate([with_prev(v_cur[kh][sd], v_prev, 2 * kh + sd, blk) for sd in range(2)], axis=0)
            p2 = jnp.concatenate([p[blk, kh, 0], p[blk, kh, 1]], axis=0)
            o_t = lax.dot_general(v2, p2, TN_DIMS, preferred_element_type=F32)
            for t in range(2):
                o_ref[blk * w:(blk + 1) * w, (2 * kh + t) * LANES:(2 * kh + t + 1) * LANES] = (
                    o_t[:, t * w:(t + 1) * w].T.astype(o_ref.dtype))
    last = slice((SWA_STEP_BLOCKS - 1) * w, SWA_STEP_BLOCKS * w)
    for kh in range(SWA_KV_HEADS):
        for sd in range(2):
            k_prev[2 * kh + sd] = k_cur[kh][sd][last]
            v_prev[2 * kh + sd] = v_cur[kh][sd][last]


def _swa_core(qkv, cos_t, sin_t, q_gain, k_gain, sinks, b, s):
    w = SWA_WINDOW
    r = SWA_STEP_BLOCKS * w
    n = s // r
    m = b * s
    half = SWA_HEAD_DIM // 2
    swapped = lambda g: jnp.concatenate([g[half:], g[:half]])
    qg = jnp.tile(q_gain.astype(F32), SWA_HEADS)[None, :]
    qgs = jnp.tile(swapped(q_gain.astype(F32)), SWA_HEADS)[None, :]
    kg = jnp.tile(k_gain.astype(F32), SWA_KV_HEADS)[None, :]
    kgs = jnp.tile(swapped(k_gain.astype(F32)), SWA_KV_HEADS)[None, :]
    consts = _swa_constants()
    row = lambda bi, ni: bi * n + ni
    const2 = lambda bi, ni: (0, 0)
    return pl.pallas_call(
        _swa_kernel,
        grid=(b, n),
        in_specs=[pl.BlockSpec(memory_space=pltpu.SMEM),
                  pl.BlockSpec((r, SWA_Q), lambda bi, ni: (row(bi, ni), 0)),
                  pl.BlockSpec((r, 2 * SWA_KV), lambda bi, ni: (row(bi, ni), SWA_Q // (2 * SWA_KV))),
                  pl.BlockSpec((r, LANES), lambda bi, ni: (row(bi, ni), 0)),
                  pl.BlockSpec((r, LANES), lambda bi, ni: (row(bi, ni), 0)),
                  pl.BlockSpec((1, SWA_Q), const2), pl.BlockSpec((1, SWA_Q), const2),
                  pl.BlockSpec((1, SWA_KV), const2), pl.BlockSpec((1, SWA_KV), const2),
                  *[pl.BlockSpec(a.shape, const2) for a in consts]],
        out_specs=pl.BlockSpec((r, SWA_Q), lambda bi, ni: (row(bi, ni), 0)),
        out_shape=jax.ShapeDtypeStruct((m, SWA_Q), BF16),
        scratch_shapes=[pltpu.VMEM((2 * SWA_KV_HEADS, w, LANES), BF16),
                        pltpu.VMEM((2 * SWA_KV_HEADS, w, LANES), BF16)],
        compiler_params=_params("parallel", "arbitrary"),
        name="swa_core",
    )(sinks.astype(F32), qkv, qkv, cos_t, sin_t, qg, qgs, kg, kgs, *consts)


GDN_STACK = 4
GDN_ROWS = GDN_STACK * GDN_CHUNK
GDN_STEP_CHUNKS = 4


def _gdn_masks():
    i = np.arange(GDN_CHUNK)[:, None]
    j = np.arange(GDN_ROWS)[None, :] % GDN_CHUNK
    r = np.arange(GDN_ROWS)
    same = (r[:, None] // GDN_CHUNK) == (r[None, :] // GDN_CHUNK)
    return jnp.asarray(i >= j, F32), jnp.asarray(i > j, F32), jnp.asarray(same, BF16)


GDN_CONV_ROWS = 256
GDN_CONV_SUB = 64


def _gdn_shift_matrix():
    n = GDN_CONV_SUB
    sh = np.zeros((3 * n, 2 * n), np.float32)
    for d in range(1, GDN_CONV):
        sh[(d - 1) * n + np.arange(n), n + np.arange(n) - d] = 1.0
    return jnp.asarray(sh, BF16)


def _gdn_conv_kernel(prev_ref, cur_ref, cw_ref, shift_ref, o_ref):
    n = GDN_CONV_SUB
    dh = GDN_HEAD_DIM
    wide = 2 * dh
    keep = jnp.where(pl.program_id(1) > 0, 1.0, 0.0).astype(BF16)
    for j in range(GDN_CONV_DIM // wide):
        cs = slice(j * wide, (j + 1) * wide)
        w = [cw_ref[d:d + 1, cs] for d in range(GDN_CONV)]
        for t in range(GDN_CONV_ROWS // n):
            if t == 0:
                window = jnp.concatenate([prev_ref[:, cs] * keep, cur_ref[0:n, cs]], axis=0)
            else:
                window = cur_ref[(t - 1) * n:(t + 1) * n, cs]
            sh = jnp.dot(shift_ref[...], window, preferred_element_type=F32)
            acc = cur_ref[t * n:(t + 1) * n, cs].astype(F32) * w[GDN_CONV - 1]
            for d in range(1, GDN_CONV):
                acc = acc + sh[(d - 1) * n:d * n, :] * w[GDN_CONV - 1 - d]
            act = acc * _sigmoid(acc)
            if j * wide < 2 * GDN_QK:
                scale = dh ** -0.5 if j * wide < GDN_QK else 1.0
                act = jnp.concatenate(
                    [act[:, h * dh:(h + 1) * dh] * (scale * lax.rsqrt(
                        jnp.sum(act[:, h * dh:(h + 1) * dh] ** 2, axis=-1, keepdims=True) + NORM_EPS))
                     for h in range(wide // dh)], axis=1)
            o_ref[t * n:(t + 1) * n, cs] = act.astype(o_ref.dtype)


def _gdn_conv(qkvz, conv_w, b, s):
    r, n = GDN_CONV_ROWS, GDN_CONV_SUB
    nt = s // r
    m = b * s
    return pl.pallas_call(
        _gdn_conv_kernel,
        grid=(b, nt),
        in_specs=[pl.BlockSpec((n, GDN_CONV_DIM), lambda bi, ti: (jnp.maximum((bi * nt + ti) * (r // n) - 1, 0), 0)),
                  pl.BlockSpec((r, GDN_CONV_DIM), lambda bi, ti: (bi * nt + ti, 0)),
                  pl.BlockSpec((GDN_CONV, GDN_CONV_DIM), lambda bi, ti: (0, 0)),
                  pl.BlockSpec((3 * n, 2 * n), lambda bi, ti: (0, 0))],
        out_specs=pl.BlockSpec((r, GDN_CONV_DIM), lambda bi, ti: (bi * nt + ti, 0)),
        out_shape=jax.ShapeDtypeStruct((m, GDN_CONV_DIM), BF16),
        compiler_params=_params("parallel", "parallel"),
        name="gdn_conv",
    )(qkvz, qkvz, conv_w, _gdn_shift_matrix())


def _gdn_kernel(qkv_ref, z_ref, ba_ref, alog_ref, dtb_ref, og_ref, mt_ref, ms_ref, bd_ref, o_ref, s_ref):
    c = GDN_CHUNK
    dh = GDN_HEAD_DIM
    nh = GDN_V_HEADS

    @pl.when(pl.program_id(1) == 0)
    def _():
        s_ref[...] = jnp.zeros_like(s_ref)

    ri = lax.broadcasted_iota(jnp.int32, (c, c), 0)
    ci = lax.broadcasted_iota(jnp.int32, (c, c), 1)
    cum = (ri >= ci).astype(F32)
    m_tril = mt_ref[...]
    m_strict = ms_ref[...]
    eye = m_tril - m_strict
    left = lax.broadcasted_iota(jnp.int32, (c, dh), 1) < c
    groups = range(nh // GDN_STACK)
    order = [[GDN_STACK * grp + r for r in (0, 2, 1, 3)] for grp in groups]
    steps = int(math.log2(c)) - 1
    dot = functools.partial(jnp.dot, preferred_element_type=F32)

    def block_diag(packed):
        return jnp.concatenate([packed] * GDN_STACK, axis=0) * bd_ref[...]

    def pack_pairs(x):
        tile = jnp.where(left, x[:c, :], x[c:, :])
        return jnp.concatenate([tile, tile], axis=1)

    def pack_cols(cols, grp):
        h0, h1, h2, h3 = order[grp]
        return jnp.concatenate([jnp.where(left, cols[h0], cols[h1]), jnp.where(left, cols[h2], cols[h3])], axis=1)

    def prepare(r0):
        tok = slice(r0, r0 + c)
        ba = ba_ref[tok, :]
        beta = pltpu.roll(_sigmoid(ba), nh, 1)
        pre = ba + dtb_ref[...]
        softplus = jnp.maximum(pre, 0.0) + jnp.log1p(jnp.exp(-jnp.abs(pre)))
        g = -jnp.exp(alog_ref[...]) * softplus
        gc = jnp.dot(cum, g, preferred_element_type=F32, precision=lax.Precision.HIGHEST)
        gc_t = gc.T
        g_col =[jnp.broadcast_to(gc[:, nh + hv:nh + hv + 1], (c, dh)) for hv in range(nh)]
        b_col = [jnp.broadcast_to(beta[:, nh + hv:nh + hv + 1], (c, dh)) for hv in range(nh)]
        eg = [jnp.exp(g_col[hv]) for hv in range(nh)]
        g_last = [g_col[hv][c - 1:c, :] for hv in range(nh)]
        qb = [qkv_ref[tok, hq * dh:(hq + 1) * dh] for hq in range(GDN_QK_HEADS)]
        kb = [qkv_ref[tok, GDN_QK + hq * dh:GDN_QK + (hq + 1) * dh] for hq in range(GDN_QK_HEADS)]
        q_h = [t.astype(F32) for t in qb]
        k_h = [t.astype(F32) for t in kb]

        low, attn = [], []
        for grp in groups:
            kab = jnp.concatenate([kb[2 * grp], kb[2 * grp + 1]], axis=0)
            qab = jnp.concatenate([qb[2 * grp], qb[2 * grp + 1]], axis=0)
            kk = pack_pairs(lax.dot_general(kab, kab, NT_DIMS, preferred_element_type=F32))
            qk = pack_pairs(lax.dot_general(qab, kab, NT_DIMS, preferred_element_type=F32))
            g_rows = jnp.concatenate([gc_t[nh + hv:nh + hv + 1, :] for hv in order[grp]], axis=1)
            decay = jnp.exp((pack_cols(g_col, grp) - g_rows) * m_tril) * m_tril
            low.append(kk * decay * pack_cols(b_col, grp) * m_strict)
            attn.append((qk * decay).astype(BF16))
        return dict(g_col=g_col, b_col=b_col, eg=eg, g_last=g_last, q_h=q_h, k_h=k_h, low=low, attn=attn)

    def inverse(low):
        idx = range(len(low))
        t_inv = [eye - low[i] for i in idx]
        pw = [low[i].astype(BF16) for i in idx]
        pw = [dot(pw[i], block_diag(pw[i])).astype(BF16) for i in idx]
        for it in range(steps):
            rhs = [block_diag(pw[i]) for i in idx]
            if it + 1 < steps:
                both = [dot(jnp.concatenate([pw[i], t_inv[i].astype(BF16)], axis=0), rhs[i]) for i in idx]
                pw = [both[i][:c].astype(BF16) for i in idx]
                t_inv = [t_inv[i] + both[i][c:] for i in idx]
            else:
                t_inv = [t_inv[i] + dot(t_inv[i].astype(BF16), rhs[i]) for i in idx]
        return t_inv

    def recur(r0, p, t_inv):
        tok = slice(r0, r0 + c)
        g_col, b_col, eg, g_last, q_h, k_h = p["g_col"], p["b_col"], p["eg"], p["g_last"], p["q_h"], p["k_h"]
        attn = p["attn"]
        uw = []
        for grp in groups:
            rhs = jnp.concatenate(
                [jnp.concatenate([qkv_ref[tok, 2 * GDN_QK + hv * dh:2 * GDN_QK + (hv + 1) * dh].astype(F32) * b_col[hv],
                                  k_h[hv // GDN_REP] * (b_col[hv] * eg[hv])], axis=1) for hv in order[grp]],
                axis=0).astype(BF16)
            uw.append(dot(block_diag(t_inv[grp].astype(BF16)), rhs))

        states = [s_ref[hv] for hv in range(nh)]
        sbs = [st.astype(BF16) for st in states]
        vnb = []
        for grp in groups:
            v_new = jnp.concatenate(
                [uw[grp][r * c:(r + 1) * c, :dh] - dot(uw[grp][r * c:(r + 1) * c, dh:].astype(BF16), sbs[hv])
                 for r, hv in enumerate(order[grp])], axis=0)
            vnb.append(v_new.astype(BF16))
        o_intra = [dot(block_diag(attn[grp]), vnb[grp]) for grp in groups]
        for grp in groups:
            for r, hv in enumerate(order[grp]):
                blk = slice(r * c, (r + 1) * c)
                q_in = (q_h[hv // GDN_REP] * eg[hv]).astype(BF16)
                k_out = (k_h[hv // GDN_REP] * jnp.exp(g_last[hv] - g_col[hv])).astype(BF16)
                o = o_intra[grp][blk] + dot(q_in, sbs[hv])
                s_ref[hv] = states[hv] * jnp.exp(g_last[hv]) + lax.dot_general(
                    k_out, vnb[grp][blk], TN_DIMS, preferred_element_type=F32)
                zz = z_ref[tok, hv * dh:(hv + 1) * dh].astype(F32)
                o_ref[tok, hv * dh:(hv + 1) * dh] = (
                    _rms(o, og_ref[...]) * (zz * _sigmoid(zz))).astype(o_ref.dtype)

    prepared = [prepare(t * c) for t in range(GDN_STEP_CHUNKS)]
    t_inv = inverse([low for p in prepared for low in p["low"]])
    for t, p in enumerate(prepared):
        recur(t * c, p, t_inv[t * len(groups):(t + 1) * len(groups)])


def _gdn_core(qkv, qkvz, ba, alog_row, dtb_row, out_gain, b, s):
    c = GDN_STEP_CHUNKS * GDN_CHUNK
    n = s // c
    m = b * s
    m_tril, m_strict, block_sel = _gdn_masks()
    row = lambda bi, ni: bi * n + ni
    const2 = lambda bi, ni: (0, 0)
    return pl.pallas_call(
        _gdn_kernel,
        grid=(b, n),
        in_specs=[pl.BlockSpec((c, GDN_CONV_DIM), lambda bi, ni: (row(bi, ni), 0)),
                  pl.BlockSpec((c, GDN_V), lambda bi, ni: (row(bi, ni), GDN_CONV_DIM // GDN_V)),
                  pl.BlockSpec((c, LANES), lambda bi, ni: (row(bi, ni), 0)),
                  pl.BlockSpec((1, LANES), const2),
                  pl.BlockSpec((1, LANES), const2),
                  pl.BlockSpec((1, GDN_HEAD_DIM), const2),
                  pl.BlockSpec((GDN_CHUNK, GDN_ROWS), const2),
                  pl.BlockSpec((GDN_CHUNK, GDN_ROWS), const2),
                  pl.BlockSpec((GDN_ROWS, GDN_ROWS), const2)],
        out_specs=pl.BlockSpec((c, GDN_V), lambda bi, ni: (row(bi, ni), 0)),
        out_shape=jax.ShapeDtypeStruct((m, GDN_V), BF16),
        scratch_shapes=[pltpu.VMEM((GDN_V_HEADS, GDN_HEAD_DIM, GDN_HEAD_DIM), F32)],
        compiler_params=_params("parallel", "arbitrary"),
        name="gdn_core",
    )(qkv, qkvz, ba, alog_row, dtb_row, out_gain, m_tril, m_strict, block_sel)


def kernel(x, positions, norm_mix, norm_mlp, w_up, w_down, ret_w_in, ret_w_out, swa_w_in, swa_q_gain,
           swa_k_gain, swa_sinks, swa_w_out, gdn_w_in, gdn_conv_w, gdn_a_log, gdn_dt_bias, gdn_out_gain,
           gdn_w_out):
    b, s, d = x.shape
    m = b * s
    depth = norm_mix.shape[0]
    x2d = x.reshape(m, d)
    pos_col = positions.reshape(m, 1).astype(F32)
    ret_cos, ret_sin = _rope_tables(pos_col, RET_DK, RET_DK)
    swa_cos, swa_sin = _rope_tables(pos_col, SWA_HEAD_DIM, LANES)

    for i in range(depth):
        mixer, j = i % 3, i // 3
        gain = norm_mix[i][None, :]
        gate = dict()
        if mixer == 0:
            proj = _inproj(x2d, gain, ret_w_in[j].astype(BF16), tn=1024)
            y = _retention_core(proj, ret_cos, ret_sin, b, s)
            w_out = ret_w_out[j]
            gate_cols = (2 * RET_QK + RET_V) // RET_V
            gate = dict(gate=_ret_gate, gate_pieces=RET_HEADS, gate_args=(proj,),
                        gate_specs=lambda tm, idx: (pl.BlockSpec((tm, RET_V), lambda r: (idx(r), gate_cols)),))
        elif mixer == 1:
            proj = _inproj(x2d, gain, swa_w_in[j].astype(BF16), tn=SWA_Q + 2 * SWA_KV)
            y = _swa_core(proj, swa_cos, swa_sin, swa_q_gain[j], swa_k_gain[j], swa_sinks[j], b, s)
            w_out = swa_w_out[j]
        else:
            n_main = GDN_CONV_DIM + GDN_V
            w_main = gdn_w_in[j][:, :n_main].astype(BF16)
            w_gate = jnp.pad(gdn_w_in[j][:, n_main:], ((0, 0), (0, LANES - 2 * GDN_V_HEADS))).astype(BF16)
            proj, ba = _inproj(x2d, gain, w_main, tn=1024, w_aux=w_gate)
            lane_pad = (GDN_V_HEADS, LANES - 2 * GDN_V_HEADS)
            alog_row = jnp.pad(gdn_a_log[j].astype(F32), lane_pad)[None, :]
            dtb_row = jnp.pad(gdn_dt_bias[j].astype(F32), lane_pad)[None, :]
            qkv_act = _gdn_conv(proj, gdn_conv_w[j].astype(F32), b, s)
            y = _gdn_core(qkv_act, proj, ba, alog_row, dtb_row, gdn_out_gain[j].astype(F32)[None, :], b, s)
            w_out = gdn_w_out[j]
        x2d = _outproj_mlp(x2d, y, w_out.astype(BF16), norm_mlp[i][None, :],
                           w_up[i].astype(BF16), w_down[i].astype(BF16), **gate)
    return x2d.reshape(b, s, d)
```

```python
import functools
import math

import numpy as np
import jax
import jax.numpy as jnp
from jax import lax
from jax.experimental import pallas as pl
from jax.experimental.pallas import tpu as pltpu

F32 = jnp.float32
BF16 = jnp.bfloat16

D_MODEL = 1024
D_FF = 4 * D_MODEL
NORM_EPS = 1e-6
ROPE_THETA = 10000.0

RET_HEADS = 4
RET_DK = 256
RET_DV = 512
RET_QK = RET_HEADS * RET_DK
RET_V = RET_HEADS * RET_DV
RET_CHUNK = 256

SWA_HEADS = 16
SWA_KV_HEADS = 4
SWA_HEAD_DIM = 64
SWA_GROUP = SWA_HEADS // SWA_KV_HEADS
SWA_WINDOW = 128
SWA_STEP_BLOCKS = 4
SWA_Q = SWA_HEADS * SWA_HEAD_DIM
SWA_KV = SWA_KV_HEADS * SWA_HEAD_DIM

GDN_QK_HEADS = 8
GDN_V_HEADS = 16
GDN_HEAD_DIM = 128
GDN_QK = GDN_QK_HEADS * GDN_HEAD_DIM
GDN_V = GDN_V_HEADS * GDN_HEAD_DIM
GDN_CONV_DIM = 2 * GDN_QK + GDN_V
GDN_CONV = 4
GDN_CHUNK = 64
GDN_REP = GDN_V_HEADS // GDN_QK_HEADS

LANES = 128
VMEM_LIMIT = 56 * 1024 * 1024
PROJ_ROWS = 512

LOG2_E = math.log2(math.e)

NT_DIMS = (((1,), (1,)), ((), ()))
TN_DIMS = (((0,), (0,)), ((), ()))


def _params(*sem):
    return pltpu.CompilerParams(dimension_semantics=sem, vmem_limit_bytes=VMEM_LIMIT)


def _sigmoid(x):
    return 1.0 / (1.0 + jnp.exp2(x * -LOG2_E))


def _rms(x, gain_row):
    ms = jnp.mean(x * x, axis=-1, keepdims=True)
    return x * lax.rsqrt(ms + NORM_EPS) * gain_row


def _rope_table_kernel(pos_ref, inv_ref, sign_ref, cos_ref, sin_ref):
    ang = pos_ref[...] * inv_ref[...]
    fold = cos_ref.shape[1] // ang.shape[1]
    cos_ref[...] = jnp.concatenate([jnp.cos(ang)] * fold, axis=1)
    sin_ref[...] = jnp.concatenate([jnp.sin(ang)] * fold, axis=1) * sign_ref[...]


def _rope_tables(pos_col, head_dim, width):
    m = pos_col.shape[0]
    half = head_dim // 2
    inv = ROPE_THETA ** (-np.arange(0, head_dim, 2, dtype=np.float32) / head_dim)
    reps = width // head_dim
    inv_row = np.tile(np.concatenate([inv, inv]), reps)[None, :].astype(np.float32)
    sign_row = np.tile(np.concatenate([-np.ones(half), np.ones(half)]), reps)[None, :].astype(np.float32)
    uniq = max(half, LANES) if half % LANES == 0 else width
    inv_row = inv_row[:, :uniq]
    tm = min(m, 1024)
    return pl.pallas_call(
        _rope_table_kernel,
        grid=(m // tm,),
        in_specs=[pl.BlockSpec((tm, 1), lambda i: (i, 0)),
                  pl.BlockSpec((1, uniq), lambda i: (0, 0)),
                  pl.BlockSpec((1, width), lambda i: (0, 0))],
        out_specs=[pl.BlockSpec((tm, width), lambda i: (i, 0)),
                   pl.BlockSpec((tm, width), lambda i: (i, 0))],
        out_shape=[jax.ShapeDtypeStruct((m, width), F32)] * 2,
        compiler_params=_params("parallel"),
        name="rope_tables",
    )(pos_col, jnp.asarray(inv_row), jnp.asarray(sign_row))


def _resident(shape):
    return pl.BlockSpec(shape, lambda i: (0,) * len(shape), pipeline_mode=pl.Buffered(1))


def _inproj_kernel(tn, x_ref, g_ref, w_ref, o_ref):
    h = _rms(x_ref[...], g_ref[...]).astype(BF16)
    for j in range(o_ref.shape[1] // tn):
        cols = slice(j * tn, (j + 1) * tn)
        o_ref[:, cols] = jnp.dot(h, w_ref[:, cols], preferred_element_type=F32).astype(o_ref.dtype)


def _inproj_aux_kernel(tn, x_ref, g_ref, w_ref, wa_ref, o_ref, oa_ref):
    h = _rms(x_ref[...], g_ref[...]).astype(BF16)
    oa_ref[...] = jnp.dot(h, wa_ref[...], preferred_element_type=F32)
    for j in range(o_ref.shape[1] // tn):
        cols = slice(j * tn, (j + 1) * tn)
        o_ref[:, cols] = jnp.dot(h, w_ref[:, cols], preferred_element_type=F32).astype(o_ref.dtype)


def _inproj(x2d, gain, w, tn, w_aux=None):
    m, d = x2d.shape
    n = w.shape[1]
    tm = min(m, PROJ_ROWS)
    x_spec = pl.BlockSpec((tm, d), lambda i: (i, 0))
    o_spec = pl.BlockSpec((tm, n), lambda i: (i, 0))
    if w_aux is None:
        return pl.pallas_call(
            functools.partial(_inproj_kernel, tn), grid=(m // tm,),
            in_specs=[x_spec, _resident((1, d)), _resident((d, n))], out_specs=o_spec,
            out_shape=jax.ShapeDtypeStruct((m, n), BF16),
            compiler_params=_params("parallel"),
            name="inproj",
        )(x2d, gain, w)
    na = w_aux.shape[1]
    return pl.pallas_call(
        functools.partial(_inproj_aux_kernel, tn), grid=(m // tm,),
        in_specs=[x_spec, _resident((1, d)), _resident((d, n)), _resident((d, na))],
        out_specs=[o_spec, pl.BlockSpec((tm, na), lambda i: (i, 0))],
        out_shape=[jax.ShapeDtypeStruct((m, n), BF16), jax.ShapeDtypeStruct((m, na), F32)],
        compiler_params=_params("parallel"),
        name="inproj_aux",
    )(x2d, gain, w, w_aux)


def _ret_gate(p, o_ref, g_ref):
    cols = slice(p * RET_DV, (p + 1) * RET_DV)
    o = o_ref[:, cols].astype(F32)
    mu = jnp.mean(o, axis=-1, keepdims=True)
    oc = o - mu
    var = jnp.mean(oc * oc, axis=-1, keepdims=True)
    g = g_ref[:, cols].astype(F32)
    return cols, (oc * lax.rsqrt(var + NORM_EPS) * (g * _sigmoid(g))).astype(BF16)


def _mlp_chunks(tf, h, wu_ref, wd_ref, o_ref, after_chunk=None):
    for j in range(wu_ref.shape[1] // tf):
        a = jnp.dot(h, wu_ref[:, j * tf:(j + 1) * tf], preferred_element_type=F32)
        a = jnp.maximum(a, 0.0)
        a = (a * a).astype(BF16)
        o_ref[...] += jnp.dot(a, wd_ref[j * tf:(j + 1) * tf, :], preferred_element_type=F32)
        if after_chunk is not None:
            after_chunk(j)


def _outproj_mlp_kernel(tf, x_ref, y_ref, wo_ref, g_ref, wu_ref, wd_ref, o_ref):
    o_ref[...] = x_ref[...] + jnp.dot(y_ref[...], wo_ref[...], preferred_element_type=F32)
    _mlp_chunks(tf, _rms(o_ref[...], g_ref[...]).astype(BF16), wu_ref, wd_ref, o_ref)


def _gated_outproj_mlp_kernel(tf, gate, n_pieces, n_gate, x_ref, *refs):
    gate_refs, (wo_ref, g_ref, wu_ref, wd_ref, o_ref, x1_buf) = refs[:n_gate], refs[n_gate:]

    @pl.when(pl.program_id(0) == 0)
    def _():
        x1_buf[...] = jnp.zeros_like(x1_buf)

    x1 = x1_buf[...]
    o_ref[...] = x1
    every = (wu_ref.shape[1] // tf) // n_pieces

    def outproj_piece(j):
        if (j + 1) % every:
            return
        p = j // every
        cols, y = gate(p, *gate_refs)
        part = jnp.dot(y, wo_ref[cols, :], preferred_element_type=F32)
        x1_buf[...] = (x_ref[...] if p == 0 else x1_buf[...]) + part

    _mlp_chunks(tf, _rms(x1, g_ref[...]).astype(BF16), wu_ref, wd_ref, o_ref, outproj_piece)


def _outproj_mlp(x2d, y, w_out, gain, w_up, w_down, gate=None, gate_pieces=1, gate_args=(), gate_specs=(),
                 tf=512):
    m, d = x2d.shape
    ky = y.shape[1]
    ff = w_up.shape[1]
    tm = min(m, PROJ_ROWS)
    nt = m // tm
    weights = [_resident((ky, d)), _resident((1, d)), _resident((d, ff)), _resident((ff, d))]
    if gate is None:
        return pl.pallas_call(
            functools.partial(_outproj_mlp_kernel, tf),
            grid=(nt,),
            in_specs=[pl.BlockSpec((tm, d), lambda i: (i, 0)), pl.BlockSpec((tm, ky), lambda i: (i, 0)), *weights],
            out_specs=pl.BlockSpec((tm, d), lambda i: (i, 0)),
            out_shape=jax.ShapeDtypeStruct((m, d), F32),
            compiler_params=_params("parallel"),
            name="outproj_mlp",
        )(x2d, y, w_out, gain, w_up, w_down)
    done = lambda i: jnp.maximum(i - 1, 0)
    ahead = lambda i: jnp.minimum(i, nt - 1)
    return pl.pallas_call(
        functools.partial(_gated_outproj_mlp_kernel, tf, gate, gate_pieces, 1 + len(gate_args)),
        grid=(nt + 1,),
        in_specs=[pl.BlockSpec((tm, d), lambda i: (ahead(i), 0)),
                  pl.BlockSpec((tm, ky), lambda i: (ahead(i), 0)),
                  *gate_specs(tm, ahead), *weights],
        out_specs=pl.BlockSpec((tm, d), lambda i: (done(i), 0)),
        out_shape=jax.ShapeDtypeStruct((m, d), F32),
        scratch_shapes=[pltpu.VMEM((tm, d), F32)],
        compiler_params=_params("arbitrary"),
        name="gated_outproj_mlp",
    )(x2d, y, *gate_args, w_out, gain, w_up, w_down)


def _ret_constants():
    c = RET_CHUNK
    idx = np.arange(c, dtype=np.float64)
    log_gamma = np.log1p(-(2.0 ** (-5.0 - np.arange(RET_HEADS, dtype=np.float64))))
    diff = idx[:, None] - idx[None, :]
    scale = RET_DK ** -0.5
    dmat = np.where(diff >= 0, np.exp(log_gamma[:, None, None] * np.maximum(diff, 0.0)), 0.0) * scale
    qd = np.exp(log_gamma[:, None] * (idx + 1.0))
    kd = np.exp(log_gamma[:, None] * (c - 1.0 - idx)) * scale
    qd = np.broadcast_to(qd[:, :, None], (RET_HEADS, c, RET_DK))
    kd = np.broadcast_to(kd[:, :, None], (RET_HEADS, c, RET_DK))
    chunk_decay = [float(v) for v in np.exp(log_gamma * c)]
    return (jnp.asarray(dmat, F32), jnp.asarray(qd, F32), jnp.asarray(kd, F32), chunk_decay)


def _ret_kernel(chunk_decay, q_ref, k_ref, v_ref, cos_ref, sin_ref, dmat_ref, qd_ref, kd_ref,
                o_ref, s_ref):
    @pl.when(pl.program_id(1) == 0)
    def _():
        s_ref[...] = jnp.zeros_like(s_ref)

    cos = cos_ref[...]
    sin = sin_ref[...]
    half = RET_DK // 2

    def rope(t):
        return t * cos + jnp.concatenate([t[:, half:], t[:, :half]], axis=1) * sin

    heads = range(RET_HEADS)
    dot = functools.partial(jnp.dot, preferred_element_type=F32)
    qr = [rope(q_ref[:, h * RET_DK:(h + 1) * RET_DK].astype(F32)) for h in heads]
    kr = [rope(k_ref[:, h * RET_DK:(h + 1) * RET_DK].astype(F32)) for h in heads]
    v = [v_ref[:, h * RET_DV:(h + 1) * RET_DV] for h in heads]
    scores = [lax.dot_general(qr[h].astype(BF16), kr[h].astype(BF16), NT_DIMS, preferred_element_type=F32)
              * dmat_ref[h] for h in heads]
    state = [s_ref[h] for h in heads]
    o_cross = [dot((qr[h] * qd_ref[h]).astype(BF16), state[h].astype(BF16)) for h in heads]
    o = [o_cross[h] + dot(scores[h].astype(BF16), v[h]) for h in heads]
    for h in heads:
        kdec = (kr[h] * kd_ref[h]).astype(BF16)
        s_ref[h] = state[h] * chunk_decay[h] + lax.dot_general(kdec, v[h], TN_DIMS, preferred_element_type=F32)
    for h in heads:
        o_ref[:, h * RET_DV:(h + 1) * RET_DV] = o[h].astype(o_ref.dtype)


def _retention_core(qkvg, cos_t, sin_t, b, s):
    c = RET_CHUNK
    n = s // c
    m = b * s
    dmat, qd, kd, chunk_decay = _ret_constants()
    row = lambda bi, ni: bi * n + ni
    const3 = lambda bi, ni: (0, 0, 0)
    return pl.pallas_call(
        functools.partial(_ret_kernel, chunk_decay),
        grid=(b, n),
        in_specs=[pl.BlockSpec((c, RET_QK), lambda bi, ni: (row(bi, ni), 0)),
                  pl.BlockSpec((c, RET_QK), lambda bi, ni: (row(bi, ni), 1)),
                  pl.BlockSpec((c, RET_V), lambda bi, ni: (row(bi, ni), 1)),
                  pl.BlockSpec((c, RET_DK), lambda bi, ni: (row(bi, ni), 0)),
                  pl.BlockSpec((c, RET_DK), lambda bi, ni: (row(bi, ni), 0)),
                  pl.BlockSpec((RET_HEADS, c, c), const3),
                  pl.BlockSpec((RET_HEADS, c, RET_DK), const3),
                  pl.BlockSpec((RET_HEADS, c, RET_DK), const3)],
        out_specs=pl.BlockSpec((c, RET_V), lambda bi, ni: (row(bi, ni), 0)),
        out_shape=jax.ShapeDtypeStruct((m, RET_V), BF16),
        scratch_shapes=[pltpu.VMEM((RET_HEADS, RET_DK, RET_DV), F32)],
        compiler_params=_params("parallel", "arbitrary"),
        name="retention_core",
    )(qkvg, qkvg, qkvg, cos_t, sin_t, dmat, qd, kd)


def _swa_constants():
    dh, half = SWA_HEAD_DIM, SWA_HEAD_DIM // 2
    i = np.arange(2 * LANES)
    swap = (i[:, None] == ((i[None, :] // dh) * dh + (i[None, :] % dh + half) % dh))
    seg_q = (np.arange(SWA_Q)[:, None] // dh) == np.arange(LANES)[None, :]
    seg_k = (np.arange(SWA_KV)[:, None] // dh) == np.arange(LANES)[None, :]
    bf = lambda a: jnp.asarray(a, BF16)
    return bf(swap), bf(seg_q), bf(seg_k), bf(seg_q.T * (dh ** -0.5)), bf(seg_k.T)


def _swa_kernel(sink_ref, q_ref, kv_ref, cos_ref, sin_ref, qg_ref, qgs_ref, kg_ref, kgs_ref,
                swap_ref, segq_ref, segk_ref, expq_ref, expk_ref, o_ref, k_prev, v_prev):
    w = SWA_WINDOW
    ni = pl.program_id(1)
    dot = functools.partial(jnp.dot, preferred_element_type=F32)

    @pl.when(ni == 0)
    def _():
        k_prev[...] = jnp.zeros_like(k_prev)
        v_prev[...] = jnp.zeros_like(v_prev)

    cos = cos_ref[...]
    sin = sin_ref[...]

    def normed_rope(xb, gain_ref, gain_swapped_ref, seg_ref, exp_ref):
        width = xb.shape[1]
        reps = width // LANES
        xf = xb.astype(F32)
        partner = jnp.concatenate(
            [dot(xb[:, t * 2 * LANES:(t + 1) * 2 * LANES], swap_ref[...]) for t in range(reps // 2)], axis=1)
        rot = (xf * (jnp.concatenate([cos] * reps, axis=1) * gain_ref[...])
               + partner * (jnp.concatenate([sin] * reps, axis=1) * gain_swapped_ref[...]))
        ss = dot((xf * xf).astype(BF16), seg_ref[...])
        inv = lax.rsqrt(ss * (1.0 / SWA_HEAD_DIM) + NORM_EPS)
        inv_hi = inv.astype(BF16)
        inv_lo = (inv - inv_hi.astype(F32)).astype(BF16)
        return rot * (dot(inv_hi, exp_ref[...]) + dot(inv_lo, exp_ref[...]))

    q_n = normed_rope(q_ref[...], qg_ref, qgs_ref, segq_ref, expq_ref).astype(BF16)
    k_n = normed_rope(kv_ref[:, :SWA_KV], kg_ref, kgs_ref, segk_ref, expk_ref)
    v_f = kv_ref[:, SWA_KV:].astype(F32)

    left = lax.broadcasted_iota(jnp.int32, (SWA_STEP_BLOCKS * w, LANES), 1) < SWA_HEAD_DIM

    def placements(x):
        out = []
        for kh in range(SWA_KV_HEADS):
            tile = x[:, (kh // 2) * LANES:(kh // 2 + 1) * LANES]
            moved = pltpu.roll(tile, SWA_HEAD_DIM, 1)
            in_left, in_right = (tile, moved) if kh % 2 == 0 else (moved, tile)
            out.append([jnp.where(left, in_left, 0.0).astype(BF16), jnp.where(left, 0.0, in_right).astype(BF16)])
        return out

    k_cur = placements(k_n)
    v_cur = placements(v_f)

    key = lax.broadcasted_iota(jnp.int32, (2 * w, 2 * w), 0)
    qry = lax.broadcasted_iota(jnp.int32, (2 * w, 2 * w), 1) % w
    rel = qry + w - key
    band = (rel >= 0) & (rel < w)
    first_key = jnp.where(ni > 0, 0, w)
    valid = [band & (key >= first_key)] + [band] * (SWA_STEP_BLOCKS - 1)

    def with_prev(cur, prev_ref, slot, blk):
        if blk == 0:
            return jnp.concatenate([prev_ref[slot], cur[:w]], axis=0)
        return cur[(blk - 1) * w:(blk + 1) * w]

    pairs = [(blk, kh, sd) for blk in range(SWA_STEP_BLOCKS) for kh in range(SWA_KV_HEADS) for sd in range(2)]
    sc, sink = {}, {}
    for blk, kh, sd in pairs:
        k2 = with_prev(k_cur[kh][sd], k_prev, 2 * kh + sd, blk)
        q2 = jnp.concatenate([q_n[blk * w:(blk + 1) * w, (2 * kh + t) * LANES:(2 * kh + t + 1) * LANES]
                              for t in range(2)], axis=0)
        sc[blk, kh, sd] = lax.dot_general(k2, q2, NT_DIMS, preferred_element_type=F32)
        heads = [SWA_GROUP * kh + 2 * t + sd for t in range(2)]
        sink[blk, kh, sd] = jnp.concatenate([jnp.full((1, w), sink_ref[h], F32) for h in heads], axis=1)
    sc = {pr: jnp.where(valid[pr[0]], sc[pr], -jnp.inf) for pr in pairs}
    mx = {pr: jnp.maximum(jnp.max(sc[pr], axis=0, keepdims=True), sink[pr]) for pr in pairs}
    p = {pr: jnp.exp(sc[pr] - mx[pr]) for pr in pairs}
    den = {pr: jnp.sum(p[pr], axis=0, keepdims=True) + jnp.exp(sink[pr] - mx[pr]) for pr in pairs}
    p = {pr: (p[pr] * (1.0 / den[pr])).astype(BF16) for pr in pairs}
    for blk in range(SWA_STEP_BLOCKS):
        for kh in range(SWA_KV_HEADS):
            v2 = jnp.concatenate([with_prev(v_cur[kh][sd], v_prev, 2 * kh + sd, blk) for sd in range(2)], axis=0)
            p2 = jnp.concatenate([p[blk, kh, 0], p[blk, kh, 1]], axis=0)
            o_t = lax.dot_general(v2, p2, TN_DIMS, preferred_element_type=F32)
            for t in range(2):
                o_ref[blk * w:(blk + 1) * w, (2 * kh + t) * LANES:(2 * kh + t + 1) * LANES] = (
                    o_t[:, t * w:(t + 1) * w].T.astype(o_ref.dtype))
    last = slice((SWA_STEP_BLOCKS - 1) * w, SWA_STEP_BLOCKS * w)
    for kh in range(SWA_KV_HEADS):
        for sd in range(2):
            k_prev[2 * kh + sd] = k_cur[kh][sd][last]
            v_prev[2 * kh + sd] = v_cur[kh][sd][last]


def _swa_core(qkv, cos_t, sin_t, q_gain, k_gain, sinks, b, s):
    w = SWA_WINDOW
    r = SWA_STEP_BLOCKS * w
    n = s // r
    m = b * s
    half = SWA_HEAD_DIM // 2
    swapped = lambda g: jnp.concatenate([g[half:], g[:half]])
    qg = jnp.tile(q_gain.astype(F32), SWA_HEADS)[None, :]
    qgs = jnp.tile(swapped(q_gain.astype(F32)), SWA_HEADS)[None, :]
    kg = jnp.tile(k_gain.astype(F32), SWA_KV_HEADS)[None, :]
    kgs = jnp.tile(swapped(k_gain.astype(F32)), SWA_KV_HEADS)[None, :]
    consts = _swa_constants()
    row = lambda bi, ni: bi * n + ni
    const2 = lambda bi, ni: (0, 0)
    return pl.pallas_call(
        _swa_kernel,
        grid=(b, n),
        in_specs=[pl.BlockSpec(memory_space=pltpu.SMEM),
                  pl.BlockSpec((r, SWA_Q), lambda bi, ni: (row(bi, ni), 0)),
                  pl.BlockSpec((r, 2 * SWA_KV), lambda bi, ni: (row(bi, ni), SWA_Q // (2 * SWA_KV))),
                  pl.BlockSpec((r, LANES), lambda bi, ni: (row(bi, ni), 0)),
                  pl.BlockSpec((r, LANES), lambda bi, ni: (row(bi, ni), 0)),
                  pl.BlockSpec((1, SWA_Q), const2), pl.BlockSpec((1, SWA_Q), const2),
                  pl.BlockSpec((1, SWA_KV), const2), pl.BlockSpec((1, SWA_KV), const2),
                  *[pl.BlockSpec(a.shape, const2) for a in consts]],
        out_specs=pl.BlockSpec((r, SWA_Q), lambda bi, ni: (row(bi, ni), 0)),
        out_shape=jax.ShapeDtypeStruct((m, SWA_Q), BF16),
        scratch_shapes=[pltpu.VMEM((2 * SWA_KV_HEADS, w, LANES), BF16),
                        pltpu.VMEM((2 * SWA_KV_HEADS, w, LANES), BF16)],
        compiler_params=_params("parallel", "arbitrary"),
        name="swa_core",
    )(sinks.astype(F32), qkv, qkv, cos_t, sin_t, qg, qgs, kg, kgs, *consts)


GDN_STACK = 4
GDN_ROWS = GDN_STACK * GDN_CHUNK
GDN_STEP_CHUNKS = 4


def _gdn_masks():
    i = np.arange(GDN_CHUNK)[:, None]
    j = np.arange(GDN_ROWS)[None, :] % GDN_CHUNK
    r = np.arange(GDN_ROWS)
    same = (r[:, None] // GDN_CHUNK) == (r[None, :] // GDN_CHUNK)
    return jnp.asarray(i >= j, F32), jnp.asarray(i > j, F32), jnp.asarray(same, BF16)


GDN_CONV_ROWS = 256
GDN_CONV_SUB = 64


def _gdn_shift_matrix():
    n = GDN_CONV_SUB
    sh = np.zeros((3 * n, 2 * n), np.float32)
    for d in range(1, GDN_CONV):
        sh[(d - 1) * n + np.arange(n), n + np.arange(n) - d] = 1.0
    return jnp.asarray(sh, BF16)


def _gdn_conv_kernel(prev_ref, cur_ref, cw_ref, shift_ref, ba_ref, alog_ref, dtb_ref, cum_ref,
                     o_ref, gate_ref, gct_ref):
    n = GDN_CONV_SUB
    dh = GDN_HEAD_DIM
    nh = GDN_V_HEADS
    wide = 2 * dh

    ba = ba_ref[...]
    beta = pltpu.roll(_sigmoid(ba), 2 * nh, 1)
    pre = ba + dtb_ref[...]
    softplus = jnp.maximum(pre, 0.0) + jnp.log1p(jnp.exp(-jnp.abs(pre)))
    g = -jnp.exp(alog_ref[...]) * softplus
    gc = jnp.dot(cum_ref[...], g, preferred_element_type=F32, precision=lax.Precision.HIGHEST)
    lane = lax.broadcasted_iota(jnp.int32, gc.shape, 1)
    gate_ref[...] = jnp.where(lane < 2 * nh, gc, beta)
    gct_ref[...] = gc.T

    keep = jnp.where(pl.program_id(1) > 0, 1.0, 0.0).astype(BF16)
    for j in range(GDN_CONV_DIM // wide):
        cs = slice(j * wide, (j + 1) * wide)
        w = [cw_ref[d:d + 1, cs] for d in range(GDN_CONV)]
        for t in range(GDN_CONV_ROWS // n):
            if t == 0:
                window = jnp.concatenate([prev_ref[:, cs] * keep, cur_ref[0:n, cs]], axis=0)
            else:
                window = cur_ref[(t - 1) * n:(t + 1) * n, cs]
            sh = jnp.dot(shift_ref[...], window, preferred_element_type=F32)
            acc = cur_ref[t * n:(t + 1) * n, cs].astype(F32) * w[GDN_CONV - 1]
            for d in range(1, GDN_CONV):
                acc = acc + sh[(d - 1) * n:d * n, :] * w[GDN_CONV - 1 - d]
            act = acc * _sigmoid(acc)
            if j * wide < 2 * GDN_QK:
                scale = dh ** -0.5 if j * wide < GDN_QK else 1.0
                act = jnp.concatenate(
                    [act[:, h * dh:(h + 1) * dh] * (scale * lax.rsqrt(
                        jnp.sum(act[:, h * dh:(h + 1) * dh] ** 2, axis=-1, keepdims=True) + NORM_EPS))
                     for h in range(wide // dh)], axis=1)
            o_ref[t * n:(t + 1) * n, cs] = act.astype(o_ref.dtype)


def _gdn_conv(qkvz, ba, conv_w, alog_row, dtb_row, b, s):
    r, n = GDN_CONV_ROWS, GDN_CONV_SUB
    nt = s // r
    m = b * s
    i = np.arange(r)
    cum = jnp.asarray((i[:, None] >= i[None, :]) & (i[:, None] // GDN_CHUNK == i[None, :] // GDN_CHUNK), F32)
    tile = lambda bi, ti: (bi * nt + ti, 0)
    const2 = lambda bi, ti: (0, 0)
    return pl.pallas_call(
        _gdn_conv_kernel,
        grid=(b, nt),
        in_specs=[pl.BlockSpec((n, GDN_CONV_DIM), lambda bi, ti: (jnp.maximum((bi * nt + ti) * (r // n) - 1, 0), 0)),
                  pl.BlockSpec((r, GDN_CONV_DIM), tile),
                  pl.BlockSpec((GDN_CONV, GDN_CONV_DIM), const2),
                  pl.BlockSpec((3 * n, 2 * n), const2),
                  pl.BlockSpec((r, LANES), tile),
                  pl.BlockSpec((1, LANES), const2),
                  pl.BlockSpec((1, LANES), const2),
                  pl.BlockSpec((r, r), const2)],
        out_specs=[pl.BlockSpec((r, GDN_CONV_DIM), tile),
                   pl.BlockSpec((r, LANES), tile),
                   pl.BlockSpec((LANES, r), lambda bi, ti: (0, bi * nt + ti))],
        out_shape=[jax.ShapeDtypeStruct((m, GDN_CONV_DIM), BF16),
                   jax.ShapeDtypeStruct((m, LANES), F32),
                   jax.ShapeDtypeStruct((LANES, m), F32)],
        compiler_params=_params("parallel", "parallel"),
        name="gdn_conv",
    )(qkvz, qkvz, conv_w, _gdn_shift_matrix(), ba, alog_row, dtb_row, cum)


def _gdn_kernel(qkv_ref, z_ref, gate_ref, gct_ref, og_ref, mt_ref, ms_ref, bd_ref, o_ref, s_ref):
    c = GDN_CHUNK
    dh = GDN_HEAD_DIM
    nh = GDN_V_HEADS

    @pl.when(pl.program_id(1) == 0)
    def _():
        s_ref[...] = jnp.zeros_like(s_ref)

    m_tril = mt_ref[...]
    m_strict = ms_ref[...]
    eye = m_tril - m_strict
    left = lax.broadcasted_iota(jnp.int32, (c, dh), 1) < c
    groups = range(nh // GDN_STACK)
    order = [[GDN_STACK * grp + r for r in (0, 2, 1, 3)] for grp in groups]
    steps = int(math.log2(c)) - 1
    dot = functools.partial(jnp.dot, preferred_element_type=F32)

    def block_diag(packed):
        return jnp.concatenate([packed] * GDN_STACK, axis=0) * bd_ref[...]

    def pack_pairs(x):
        tile = jnp.where(left, x[:c, :], x[c:, :])
        return jnp.concatenate([tile, tile], axis=1)

    def pack_cols(cols, grp):
        h0, h1, h2, h3 = order[grp]
        return jnp.concatenate([jnp.where(left, cols[h0], cols[h1]), jnp.where(left, cols[h2], cols[h3])], axis=1)

    def prepare(r0):
        tok = slice(r0, r0 + c)
        gates = gate_ref[tok, :]
        g_col = [jnp.broadcast_to(gates[:, nh + hv:nh + hv + 1], (c, dh)) for hv in range(nh)]
        b_col = [jnp.broadcast_to(gates[:, 2 * nh + hv:2 * nh + hv + 1], (c, dh)) for hv in range(nh)]
        eg = [jnp.exp(g_col[hv]) for hv in range(nh)]
        g_last = [g_col[hv][c - 1:c, :] for hv in range(nh)]
        qb = [qkv_ref[tok, hq * dh:(hq + 1) * dh] for hq in range(GDN_QK_HEADS)]
        kb = [qkv_ref[tok, GDN_QK + hq * dh:GDN_QK + (hq + 1) * dh] for hq in range(GDN_QK_HEADS)]
        q_h = [t.astype(F32) for t in qb]
        k_h = [t.astype(F32) for t in kb]

        low, attn = [], []
        for grp in groups:
            kab = jnp.concatenate([kb[2 * grp], kb[2 * grp + 1]], axis=0)
            qab = jnp.concatenate([qb[2 * grp], qb[2 * grp + 1]], axis=0)
            kk = pack_pairs(lax.dot_general(kab, kab, NT_DIMS, preferred_element_type=F32))
            qk = pack_pairs(lax.dot_general(qab, kab, NT_DIMS, preferred_element_type=F32))
            g_rows = jnp.concatenate([gct_ref[nh + hv:nh + hv + 1, tok] for hv in order[grp]], axis=1)
            decay = jnp.exp((pack_cols(g_col, grp) - g_rows) * m_tril) * m_tril
            low.append(kk * decay * pack_cols(b_col, grp) * m_strict)
            attn.append((qk * decay).astype(BF16))
        return dict(g_col=g_col, b_col=b_col, eg=eg, g_last=g_last, q_h=q_h, k_h=k_h, low=low, attn=attn)

    def inverse(low):
        idx = range(len(low))
        t_inv = [eye - low[i] for i in idx]
        pw = [low[i].astype(BF16) for i in idx]
        pw = [dot(pw[i], block_diag(pw[i])).astype(BF16) for i in idx]
        for it in range(steps):
            rhs = [block_diag(pw[i]) for i in idx]
            if it + 1 < steps:
                both = [dot(jnp.concatenate([pw[i], t_inv[i].astype(BF16)], axis=0), rhs[i]) for i in idx]
                pw = [both[i][:c].astype(BF16) for i in idx]
                t_inv = [t_inv[i] + both[i][c:] for i in idx]
            else:
                t_inv = [t_inv[i] + dot(t_inv[i].astype(BF16), rhs[i]) for i in idx]
        return t_inv

    def solve(r0, p, t_inv):
        tok = slice(r0, r0 + c)
        b_col, eg, k_h = p["b_col"], p["eg"], p["k_h"]
        uw = []
        for grp in groups:
            rhs = jnp.concatenate(
                [jnp.concatenate([qkv_ref[tok, 2 * GDN_QK + hv * dh:2 * GDN_QK + (hv + 1) * dh].astype(F32) * b_col[hv],
                                  k_h[hv // GDN_REP] * (b_col[hv] * eg[hv])], axis=1) for hv in order[grp]],
                axis=0).astype(BF16)
            uw.append(dot(block_diag(t_inv[grp].astype(BF16)), rhs))
        return uw

    def recur(r0, p, uw):
        tok = slice(r0, r0 + c)
        g_col, eg, g_last, q_h, k_h, attn = p["g_col"], p["eg"], p["g_last"], p["q_h"], p["k_h"], p["attn"]
        states = [s_ref[hv] for hv in range(nh)]
        sbs = [st.astype(BF16) for st in states]
        vnb, o_cross = [], {}
        for grp in groups:
            v_new = []
            for r, hv in enumerate(order[grp]):
                blk = slice(r * c, (r + 1) * c)
                q_in = (q_h[hv // GDN_REP] * eg[hv]).astype(BF16)
                on_state = dot(jnp.concatenate([uw[grp][blk, dh:].astype(BF16), q_in], axis=0), sbs[hv])
                v_new.append(uw[grp][blk, :dh] - on_state[:c])
                o_cross[hv] = on_state[c:]
            vnb.append(jnp.concatenate(v_new, axis=0).astype(BF16))
        o_intra = [dot(block_diag(attn[grp]), vnb[grp]) for grp in groups]
        for grp in groups:
            for r, hv in enumerate(order[grp]):
                blk = slice(r * c, (r + 1) * c)
                k_out = (k_h[hv // GDN_REP] * jnp.exp(g_last[hv] - g_col[hv])).astype(BF16)
                o = o_intra[grp][blk] + o_cross[hv]
                s_ref[hv] = states[hv] * jnp.exp(g_last[hv]) + lax.dot_general(
                    k_out, vnb[grp][blk], TN_DIMS, preferred_element_type=F32)
                zz = z_ref[tok, hv * dh:(hv + 1) * dh].astype(F32)
                o_ref[tok, hv * dh:(hv + 1) * dh] = (
                    _rms(o, og_ref[...]) * (zz * _sigmoid(zz))).astype(o_ref.dtype)

    prepared = [prepare(t * c) for t in range(GDN_STEP_CHUNKS)]
    t_inv = inverse([low for p in prepared for low in p["low"]])
    solved = [solve(t * c, p, t_inv[t * len(groups):(t + 1) * len(groups)]) for t, p in enumerate(prepared)]
    for t, p in enumerate(prepared):
        recur(t * c, p, solved[t])


def _gdn_core(qkv, qkvz, gates, gates_t, out_gain, b, s):
    c = GDN_STEP_CHUNKS * GDN_CHUNK
    n = s // c
    m = b * s
    m_tril, m_strict, block_sel = _gdn_masks()
    row = lambda bi, ni: bi * n + ni
    const2 = lambda bi, ni: (0, 0)
    return pl.pallas_call(
        _gdn_kernel,
        grid=(b, n),
        in_specs=[pl.BlockSpec((c, GDN_CONV_DIM), lambda bi, ni: (row(bi, ni), 0)),
                  pl.BlockSpec((c, GDN_V), lambda bi, ni: (row(bi, ni), GDN_CONV_DIM // GDN_V)),
                  pl.BlockSpec((c, LANES), lambda bi, ni: (row(bi, ni), 0)),
                  pl.BlockSpec((LANES, c), lambda bi, ni: (0, row(bi, ni))),
                  pl.BlockSpec((1, GDN_HEAD_DIM), const2),
                  pl.BlockSpec((GDN_CHUNK, GDN_ROWS), const2),
                  pl.BlockSpec((GDN_CHUNK, GDN_ROWS), const2),
                  pl.BlockSpec((GDN_ROWS, GDN_ROWS), const2)],
        out_specs=pl.BlockSpec((c, GDN_V), lambda bi, ni: (row(bi, ni), 0)),
        out_shape=jax.ShapeDtypeStruct((m, GDN_V), BF16),
        scratch_shapes=[pltpu.VMEM((GDN_V_HEADS, GDN_HEAD_DIM, GDN_HEAD_DIM), F32)],
        compiler_params=_params("parallel", "arbitrary"),
        name="gdn_core",
    )(qkv, qkvz, gates, gates_t, out_gain, m_tril, m_strict, block_sel)


def kernel(x, positions, norm_mix, norm_mlp, w_up, w_down, ret_w_in, ret_w_out, swa_w_in, swa_q_gain,
           swa_k_gain, swa_sinks, swa_w_out, gdn_w_in, gdn_conv_w, gdn_a_log, gdn_dt_bias, gdn_out_gain,
           gdn_w_out):
    b, s, d = x.shape
    m = b * s
    depth = norm_mix.shape[0]
    x2d = x.reshape(m, d)
    pos_col = positions.reshape(m, 1).astype(F32)
    ret_cos, ret_sin = _rope_tables(pos_col, RET_DK, RET_DK)
    swa_cos, swa_sin = _rope_tables(pos_col, SWA_HEAD_DIM, LANES)

    for i in range(depth):
        mixer, j = i % 3, i // 3
        gain = norm_mix[i][None, :]
        gate = dict()
        if mixer == 0:
            proj = _inproj(x2d, gain, ret_w_in[j].astype(BF16), tn=1024)
            y = _retention_core(proj, ret_cos, ret_sin, b, s)
            w_out = ret_w_out[j]
            gate_cols = (2 * RET_QK + RET_V) // RET_V
            gate = dict(gate=_ret_gate, gate_pieces=RET_HEADS, gate_args=(proj,),
                        gate_specs=lambda tm, idx: (pl.BlockSpec((tm, RET_V), lambda r: (idx(r), gate_cols)),))
        elif mixer == 1:
            proj = _inproj(x2d, gain, swa_w_in[j].astype(BF16), tn=SWA_Q + 2 * SWA_KV)
            y = _swa_core(proj, swa_cos, swa_sin, swa_q_gain[j], swa_k_gain[j], swa_sinks[j], b, s)
            w_out = swa_w_out[j]
        else:
            n_main = GDN_CONV_DIM + GDN_V
            w_main = gdn_w_in[j][:, :n_main].astype(BF16)
            w_gate = jnp.pad(gdn_w_in[j][:, n_main:], ((0, 0), (0, LANES - 2 * GDN_V_HEADS))).astype(BF16)
            proj, ba = _inproj(x2d, gain, w_main, tn=1024, w_aux=w_gate)
            lane_pad = (GDN_V_HEADS, LANES - 2 * GDN_V_HEADS)
            alog_row = jnp.pad(gdn_a_log[j].astype(F32), lane_pad)[None, :]
            dtb_row = jnp.pad(gdn_dt_bias[j].astype(F32), lane_pad)[None, :]
            qkv_act, gates, gates_t = _gdn_conv(proj, ba, gdn_conv_w[j].astype(F32), alog_row, dtb_row, b, s)
            y = _gdn_core(qkv_act, proj, gates, gates_t, gdn_out_gain[j].astype(F32)[None, :], b, s)
            w_out = gdn_w_out[j]
        x2d = _outproj_mlp(x2d, y, w_out.astype(BF16), norm_mlp[i][None, :],
                           w_up[i].astype(BF16), w_down[i].astype(BF16), **gate)
    return x2d.reshape(b, s, d)
```

```python
import functools
import math

import numpy as np
import jax
import jax.numpy as jnp
from jax import lax
from jax.experimental import pallas as pl
from jax.experimental.pallas import tpu as pltpu

F32 = jnp.float32
BF16 = jnp.bfloat16

NORM_EPS = 1e-6
ROPE_THETA = 10000.0

RET_HEADS = 4
RET_DK = 256
RET_DV = 512
RET_QK = RET_HEADS * RET_DK
RET_V = RET_HEADS * RET_DV
RET_CHUNK = 256
RET_STEP_CHUNKS = 2

SWA_HEADS = 16
SWA_KV_HEADS = 4
SWA_HEAD_DIM = 64
SWA_GROUP = SWA_HEADS // SWA_KV_HEADS
SWA_WINDOW = 128
SWA_STEP_BLOCKS = 4
SWA_Q = SWA_HEADS * SWA_HEAD_DIM
SWA_KV = SWA_KV_HEADS * SWA_HEAD_DIM

GDN_QK_HEADS = 8
GDN_V_HEADS = 16
GDN_HEAD_DIM = 128
GDN_QK = GDN_QK_HEADS * GDN_HEAD_DIM
GDN_V = GDN_V_HEADS * GDN_HEAD_DIM
GDN_CONV_DIM = 2 * GDN_QK + GDN_V
GDN_CONV = 4
GDN_CHUNK = 64
GDN_REP = GDN_V_HEADS // GDN_QK_HEADS

LANES = 128
VMEM_LIMIT = 56 * 1024 * 1024
PROJ_ROWS = 512

LOG2_E = math.log2(math.e)

NT_DIMS = (((1,), (1,)), ((), ()))
TN_DIMS = (((0,), (0,)), ((), ()))


def _params(*sem):
    return pltpu.CompilerParams(dimension_semantics=sem, vmem_limit_bytes=VMEM_LIMIT)


def _sigmoid(x):
    return 1.0 / (1.0 + jnp.exp2(x * -LOG2_E))


def _rms(x, gain_row):
    ms = jnp.mean(x * x, axis=-1, keepdims=True)
    return x * lax.rsqrt(ms + NORM_EPS) * gain_row


def _rope_table_kernel(pos_ref, inv_ref, sign_ref, cos_ref, sin_ref):
    ang = pos_ref[...] * inv_ref[...]
    fold = cos_ref.shape[1] // ang.shape[1]
    cos_ref[...] = jnp.concatenate([jnp.cos(ang)] * fold, axis=1)
    sin_ref[...] = jnp.concatenate([jnp.sin(ang)] * fold, axis=1) * sign_ref[...]


def _rope_tables(pos_col, head_dim, width):
    m = pos_col.shape[0]
    half = head_dim // 2
    inv = ROPE_THETA ** (-np.arange(0, head_dim, 2, dtype=np.float32) / head_dim)
    reps = width // head_dim
    inv_row = np.tile(np.concatenate([inv, inv]), reps)[None, :].astype(np.float32)
    sign_row = np.tile(np.concatenate([-np.ones(half), np.ones(half)]), reps)[None, :].astype(np.float32)
    uniq = max(half, LANES) if half % LANES == 0 else width
    inv_row = inv_row[:, :uniq]
    tm = min(m, 1024)
    return pl.pallas_call(
        _rope_table_kernel,
        grid=(m // tm,),
        in_specs=[pl.BlockSpec((tm, 1), lambda i: (i, 0)),
                  pl.BlockSpec((1, uniq), lambda i: (0, 0)),
                  pl.BlockSpec((1, width), lambda i: (0, 0))],
        out_specs=[pl.BlockSpec((tm, width), lambda i: (i, 0)),
                   pl.BlockSpec((tm, width), lambda i: (i, 0))],
        out_shape=[jax.ShapeDtypeStruct((m, width), F32)] * 2,
        compiler_params=_params("parallel"),
        name="rope_tables",
    )(pos_col, jnp.asarray(inv_row), jnp.asarray(sign_row))


def _resident(shape):
    return pl.BlockSpec(shape, lambda i: (0,) * len(shape), pipeline_mode=pl.Buffered(1))


def _inproj_kernel(tn, x_ref, g_ref, w_ref, o_ref):
    h = _rms(x_ref[...], g_ref[...]).astype(BF16)
    for j in range(o_ref.shape[1] // tn):
        cols = slice(j * tn, (j + 1) * tn)
        o_ref[:, cols] = jnp.dot(h, w_ref[:, cols], preferred_element_type=F32).astype(o_ref.dtype)


def _inproj_aux_kernel(tn, x_ref, g_ref, w_ref, wa_ref, o_ref, oa_ref):
    h = _rms(x_ref[...], g_ref[...]).astype(BF16)
    oa_ref[...] = jnp.dot(h, wa_ref[...], preferred_element_type=F32)
    for j in range(o_ref.shape[1] // tn):
        cols = slice(j * tn, (j + 1) * tn)
        o_ref[:, cols] = jnp.dot(h, w_ref[:, cols], preferred_element_type=F32).astype(o_ref.dtype)


def _inproj(x2d, gain, w, tn, w_aux=None):
    m, d = x2d.shape
    n = w.shape[1]
    tm = min(m, PROJ_ROWS)
    x_spec = pl.BlockSpec((tm, d), lambda i: (i, 0))
    o_spec = pl.BlockSpec((tm, n), lambda i: (i, 0))
    if w_aux is None:
        return pl.pallas_call(
            functools.partial(_inproj_kernel, tn), grid=(m // tm,),
            in_specs=[x_spec, _resident((1, d)), _resident((d, n))], out_specs=o_spec,
            out_shape=jax.ShapeDtypeStruct((m, n), BF16),
            compiler_params=_params("parallel"),
            name="inproj",
        )(x2d, gain, w)
    na = w_aux.shape[1]
    return pl.pallas_call(
        functools.partial(_inproj_aux_kernel, tn), grid=(m // tm,),
        in_specs=[x_spec, _resident((1, d)), _resident((d, n)), _resident((d, na))],
        out_specs=[o_spec, pl.BlockSpec((tm, na), lambda i: (i, 0))],
        out_shape=[jax.ShapeDtypeStruct((m, n), BF16), jax.ShapeDtypeStruct((m, na), F32)],
        compiler_params=_params("parallel"),
        name="inproj_aux",
    )(x2d, gain, w, w_aux)


def _ret_gate(p, o_ref, g_ref):
    cols = slice(p * RET_DV, (p + 1) * RET_DV)
    o = o_ref[:, cols].astype(F32)
    mu = jnp.mean(o, axis=-1, keepdims=True)
    oc = o - mu
    var = jnp.mean(oc * oc, axis=-1, keepdims=True)
    g = g_ref[:, cols].astype(F32)
    return cols, (oc * lax.rsqrt(var + NORM_EPS) * (g * _sigmoid(g))).astype(BF16)


def _mlp_chunks(tf, h, wu_ref, wd_ref, o_ref, after_chunk=None):
    for j in range(wu_ref.shape[1] // tf):
        a = jnp.dot(h, wu_ref[:, j * tf:(j + 1) * tf], preferred_element_type=F32)
        a = jnp.maximum(a, 0.0)
        a = (a * a).astype(BF16)
        o_ref[...] += jnp.dot(a, wd_ref[j * tf:(j + 1) * tf, :], preferred_element_type=F32)
        if after_chunk is not None:
            after_chunk(j)


def _outproj_mlp_kernel(tf, x_ref, y_ref, wo_ref, g_ref, wu_ref, wd_ref, o_ref):
    o_ref[...] = x_ref[...] + jnp.dot(y_ref[...], wo_ref[...], preferred_element_type=F32)
    _mlp_chunks(tf, _rms(o_ref[...], g_ref[...]).astype(BF16), wu_ref, wd_ref, o_ref)


def _gated_outproj_mlp_kernel(tf, gate, n_pieces, n_gate, x_ref, *refs):
    gate_refs, (wo_ref, g_ref, wu_ref, wd_ref, o_ref, x1_buf) = refs[:n_gate], refs[n_gate:]

    @pl.when(pl.program_id(0) == 0)
    def _():
        x1_buf[...] = jnp.zeros_like(x1_buf)

    x1 = x1_buf[...]
    o_ref[...] = x1
    every = (wu_ref.shape[1] // tf) // n_pieces

    def outproj_piece(j):
        if (j + 1) % every:
            return
        p = j // every
        cols, y = gate(p, *gate_refs)
        part = jnp.dot(y, wo_ref[cols, :], preferred_element_type=F32)
        x1_buf[...] = (x_ref[...] if p == 0 else x1_buf[...]) + part

    _mlp_chunks(tf, _rms(x1, g_ref[...]).astype(BF16), wu_ref, wd_ref, o_ref, outproj_piece)


def _outproj_mlp(x2d, y, w_out, gain, w_up, w_down, gate=None, gate_pieces=1, gate_args=(), gate_specs=(),
                 tf=512):
    m, d = x2d.shape
    ky = y.shape[1]
    ff = w_up.shape[1]
    tm = min(m, PROJ_ROWS)
    nt = m // tm
    weights = [_resident((ky, d)), _resident((1, d)), _resident((d, ff)), _resident((ff, d))]
    if gate is None:
        return pl.pallas_call(
            functools.partial(_outproj_mlp_kernel, tf),
            grid=(nt,),
            in_specs=[pl.BlockSpec((tm, d), lambda i: (i, 0)), pl.BlockSpec((tm, ky), lambda i: (i, 0)), *weights],
            out_specs=pl.BlockSpec((tm, d), lambda i: (i, 0)),
            out_shape=jax.ShapeDtypeStruct((m, d), F32),
            compiler_params=_params("parallel"),
            name="outproj_mlp",
        )(x2d, y, w_out, gain, w_up, w_down)
    done = lambda i: jnp.maximum(i - 1, 0)
    ahead = lambda i: jnp.minimum(i, nt - 1)
    return pl.pallas_call(
        functools.partial(_gated_outproj_mlp_kernel, tf, gate, gate_pieces, 1 + len(gate_args)),
        grid=(nt + 1,),
        in_specs=[pl.BlockSpec((tm, d), lambda i: (ahead(i), 0)),
                  pl.BlockSpec((tm, ky), lambda i: (ahead(i), 0)),
                  *gate_specs(tm, ahead), *weights],
        out_specs=pl.BlockSpec((tm, d), lambda i: (done(i), 0)),
        out_shape=jax.ShapeDtypeStruct((m, d), F32),
        scratch_shapes=[pltpu.VMEM((tm, d), F32)],
        compiler_params=_params("arbitrary"),
        name="gated_outproj_mlp",
    )(x2d, y, *gate_args, w_out, gain, w_up, w_down)


def _ret_constants():
    c = RET_CHUNK
    idx = np.arange(c, dtype=np.float64)
    log_gamma = np.log1p(-(2.0 ** (-5.0 - np.arange(RET_HEADS, dtype=np.float64))))
    diff = idx[:, None] - idx[None, :]
    scale = RET_DK ** -0.5
    dmat = np.where(diff >= 0, np.exp(log_gamma[:, None, None] * np.maximum(diff, 0.0)), 0.0) * scale
    qd = np.exp(log_gamma[:, None] * (idx + 1.0))
    kd = np.exp(log_gamma[:, None] * (c - 1.0 - idx)) * scale
    qd = np.broadcast_to(qd[:, :, None], (RET_HEADS, c, RET_DK))
    kd = np.broadcast_to(kd[:, :, None], (RET_HEADS, c, RET_DK))
    chunk_decay = [float(v) for v in np.exp(log_gamma * c)]
    return (jnp.asarray(dmat, F32), jnp.asarray(qd, F32), jnp.asarray(kd, F32), chunk_decay)


def _ret_kernel(chunk_decay, q_ref, k_ref, v_ref, cos_ref, sin_ref, dmat_ref, qd_ref, kd_ref,
                o_ref, s_ref):
    @pl.when(pl.program_id(1) == 0)
    def _():
        s_ref[...] = jnp.zeros_like(s_ref)

    c = RET_CHUNK
    half = RET_DK // 2
    heads = range(RET_HEADS)
    dot = functools.partial(jnp.dot, preferred_element_type=F32)

    qr, kr, v, intra = {}, {}, {}, {}
    for t in range(RET_STEP_CHUNKS):
        tok = slice(t * c, (t + 1) * c)
        cos = cos_ref[tok, :]
        sin = sin_ref[tok, :]

        def rope(x):
            return x * cos + jnp.concatenate([x[:, half:], x[:, :half]], axis=1) * sin

        for h in heads:
            qr[t, h] = rope(q_ref[tok, h * RET_DK:(h + 1) * RET_DK].astype(F32))
            kr[t, h] = rope(k_ref[tok, h * RET_DK:(h + 1) * RET_DK].astype(F32))
            v[t, h] = v_ref[tok, h * RET_DV:(h + 1) * RET_DV]
    for key in qr:
        scores = lax.dot_general(qr[key].astype(BF16), kr[key].astype(BF16), NT_DIMS,
                                 preferred_element_type=F32) * dmat_ref[key[1]]
        intra[key] = dot(scores.astype(BF16), v[key])
    for t in range(RET_STEP_CHUNKS):
        tok = slice(t * c, (t + 1) * c)
        state = [s_ref[h] for h in heads]
        o = [intra[t, h] + dot((qr[t, h] * qd_ref[h]).astype(BF16), state[h].astype(BF16)) for h in heads]
        for h in heads:
            kdec = (kr[t, h] * kd_ref[h]).astype(BF16)
            s_ref[h] = state[h] * chunk_decay[h] + lax.dot_general(kdec, v[t, h], TN_DIMS,
                                                                   preferred_element_type=F32)
        for h in heads:
            o_ref[tok, h * RET_DV:(h + 1) * RET_DV] = o[h].astype(o_ref.dtype)


def _retention_core(qkvg, cos_t, sin_t, b, s):
    c = RET_STEP_CHUNKS * RET_CHUNK
    n = s // c
    m = b * s
    dmat, qd, kd, chunk_decay = _ret_constants()
    row = lambda bi, ni: bi * n + ni
    const3 = lambda bi, ni: (0, 0, 0)
    return pl.pallas_call(
        functools.partial(_ret_kernel, chunk_decay),
        grid=(b, n),
        in_specs=[pl.BlockSpec((c, RET_QK), lambda bi, ni: (row(bi, ni), 0)),
                  pl.BlockSpec((c, RET_QK), lambda bi, ni: (row(bi, ni), 1)),
                  pl.BlockSpec((c, RET_V), lambda bi, ni: (row(bi, ni), 1)),
                  pl.BlockSpec((c, RET_DK), lambda bi, ni: (row(bi, ni), 0)),
                  pl.BlockSpec((c, RET_DK), lambda bi, ni: (row(bi, ni), 0)),
                  pl.BlockSpec((RET_HEADS, RET_CHUNK, RET_CHUNK), const3),
                  pl.BlockSpec((RET_HEADS, RET_CHUNK, RET_DK), const3),
                  pl.BlockSpec((RET_HEADS, RET_CHUNK, RET_DK), const3)],
        out_specs=pl.BlockSpec((c, RET_V), lambda bi, ni: (row(bi, ni), 0)),
        out_shape=jax.ShapeDtypeStruct((m, RET_V), BF16),
        scratch_shapes=[pltpu.VMEM((RET_HEADS, RET_DK, RET_DV), F32)],
        compiler_params=_params("parallel", "arbitrary"),
        name="retention_core",
    )(qkvg, qkvg, qkvg, cos_t, sin_t, dmat, qd, kd)


def _swa_constants():
    dh, half = SWA_HEAD_DIM, SWA_HEAD_DIM // 2
    i = np.arange(2 * LANES)
    swap = (i[:, None] == ((i[None, :] // dh) * dh + (i[None, :] % dh + half) % dh))
    seg_q = (np.arange(SWA_Q)[:, None] // dh) == np.arange(LANES)[None, :]
    seg_k = (np.arange(SWA_KV)[:, None] // dh) == np.arange(LANES)[None, :]
    bf = lambda a: jnp.asarray(a, BF16)
    return bf(swap), bf(seg_q), bf(seg_k), bf(seg_q.T * (dh ** -0.5)), bf(seg_k.T)


def _swa_kernel(sink_ref, q_ref, kv_ref, cos_ref, sin_ref, qg_ref, qgs_ref, kg_ref, kgs_ref,
                swap_ref, segq_ref, segk_ref, expq_ref, expk_ref, o_ref, k_prev, v_prev):
    w = SWA_WINDOW
    ni = pl.program_id(1)
    dot = functools.partial(jnp.dot, preferred_element_type=F32)

    @pl.when(ni == 0)
    def _():
        k_prev[...] = jnp.zeros_like(k_prev)
        v_prev[...] = jnp.zeros_like(v_prev)

    cos = cos_ref[...]
    sin = sin_ref[...]

    def normed_rope(xb, gain_ref, gain_swapped_ref, seg_ref, exp_ref):
        width = xb.shape[1]
        reps = width // LANES
        xf = xb.astype(F32)
        partner = jnp.concatenate(
            [dot(xb[:, t * 2 * LANES:(t + 1) * 2 * LANES], swap_ref[...]) for t in range(reps // 2)], axis=1)
        rot = (xf * (jnp.concatenate([cos] * reps, axis=1) * gain_ref[...])
               + partner * (jnp.concatenate([sin] * reps, axis=1) * gain_swapped_ref[...]))
        ss = dot((xf * xf).astype(BF16), seg_ref[...])
        inv = lax.rsqrt(ss * (1.0 / SWA_HEAD_DIM) + NORM_EPS)
        inv_hi = inv.astype(BF16)
        inv_lo = (inv - inv_hi.astype(F32)).astype(BF16)
        return rot * (dot(inv_hi, exp_ref[...]) + dot(inv_lo, exp_ref[...]))

    q_n = normed_rope(q_ref[...], qg_ref, qgs_ref, segq_ref, expq_ref).astype(BF16)
    k_n = normed_rope(kv_ref[:, :SWA_KV], kg_ref, kgs_ref, segk_ref, expk_ref)
    v_f = kv_ref[:, SWA_KV:].astype(F32)

    left = lax.broadcasted_iota(jnp.int32, (SWA_STEP_BLOCKS * w, LANES), 1) < SWA_HEAD_DIM

    def placements(x):
        out = []
        for kh in range(SWA_KV_HEADS):
            tile = x[:, (kh // 2) * LANES:(kh // 2 + 1) * LANES]
            moved = pltpu.roll(tile, SWA_HEAD_DIM, 1)
            in_left, in_right = (tile, moved) if kh % 2 == 0 else (moved, tile)
            out.append([jnp.where(left, in_left, 0.0).astype(BF16), jnp.where(left, 0.0, in_right).astype(BF16)])
        return out

    k_cur = placements(k_n)
    v_cur = placements(v_f)

    key = lax.broadcasted_iota(jnp.int32, (2 * w, 2 * w), 0)
    qry = lax.broadcasted_iota(jnp.int32, (2 * w, 2 * w), 1) % w
    rel = qry + w - key
    band = (rel >= 0) & (rel < w)
    first_key = jnp.where(ni > 0, 0, w)
    valid = [band & (key >= first_key)] + [band] * (SWA_STEP_BLOCKS - 1)

    def with_prev(cur, prev_ref, slot, blk):
        if blk == 0:
            return jnp.concatenate([prev_ref[slot], cur[:w]], axis=0)
        return cur[(blk - 1) * w:(blk + 1) * w]

    pairs = [(blk, kh, sd) for blk in range(SWA_STEP_BLOCKS) for kh in range(SWA_KV_HEADS) for sd in range(2)]
    sc, sink = {}, {}
    for blk, kh, sd in pairs:
        k2 = with_prev(k_cur[kh][sd], k_prev, 2 * kh + sd, blk)
        q2 = jnp.concatenate([q_n[blk * w:(blk + 1) * w, (2 * kh + t) * LANES:(2 * kh + t + 1) * LANES]
                              for t in range(2)], axis=0)
        sc[blk, kh, sd] = lax.dot_general(k2, q2, NT_DIMS, preferred_element_type=F32)
        heads = [SWA_GROUP * kh + 2 * t + sd for t in range(2)]
        sink[blk, kh, sd] = jnp.concatenate([jnp.full((1, w), sink_ref[h], F32) for h in heads], axis=1)
    sc = {pr: jnp.where(valid[pr[0]], sc[pr], -jnp.inf) for pr in pairs}
    mx = {pr: jnp.maximum(jnp.max(sc[pr], axis=0, keepdims=True), sink[pr]) for pr in pairs}
    p = {pr: jnp.exp(sc[pr] - mx[pr]) for pr in pairs}
    den = {pr: jnp.sum(p[pr], axis=0, keepdims=True) + jnp.exp(sink[pr] - mx[pr]) for pr in pairs}
    p = {pr: (p[pr] * (1.0 / den[pr])).astype(BF16) for pr in pairs}
    for blk in range(SWA_STEP_BLOCKS):
        for kh in range(SWA_KV_HEADS):
            v2 = jnp.concatenate([with_prev(v_cur[kh][sd], v_prev, 2 * kh + sd, blk) for sd in range(2)], axis=0)
            p2 = jnp.concatenate([p[blk, kh, 0], p[blk, kh, 1]], axis=0)
            o_t = lax.dot_general(v2, p2, TN_DIMS, preferred_element_type=F32)
            for t in range(2):
                o_ref[blk * w:(blk + 1) * w, (2 * kh + t) * LANES:(2 * kh + t + 1) * LANES] = (
                    o_t[:, t * w:(t + 1) * w].T.astype(o_ref.dtype))
    last = slice((SWA_STEP_BLOCKS - 1) * w, SWA_STEP_BLOCKS * w)
    for kh in range(SWA_KV_HEADS):
        for sd in range(2):
            k_prev[2 * kh + sd] = k_cur[kh][sd][last]
            v_prev[2 * kh + sd] = v_cur[kh][sd][last]


def _swa_core(qkv, cos_t, sin_t, q_gain, k_gain, sinks, b, s):
    w = SWA_WINDOW
    r = SWA_STEP_BLOCKS * w
    n = s // r
    m = b * s
    half = SWA_HEAD_DIM // 2
    swapped = lambda g: jnp.concatenate([g[half:], g[:half]])
    qg = jnp.tile(q_gain.astype(F32), SWA_HEADS)[None, :]
    qgs = jnp.tile(swapped(q_gain.astype(F32)), SWA_HEADS)[None, :]
    kg = jnp.tile(k_gain.astype(F32), SWA_KV_HEADS)[None, :]
    kgs = jnp.tile(swapped(k_gain.astype(F32)), SWA_KV_HEADS)[None, :]
    consts = _swa_constants()
    row = lambda bi, ni: bi * n + ni
    const2 = lambda bi, ni: (0, 0)
    return pl.pallas_call(
        _swa_kernel,
        grid=(b, n),
        in_specs=[pl.BlockSpec(memory_space=pltpu.SMEM),
                  pl.BlockSpec((r, SWA_Q), lambda bi, ni: (row(bi, ni), 0)),
                  pl.BlockSpec((r, 2 * SWA_KV), lambda bi, ni: (row(bi, ni), SWA_Q // (2 * SWA_KV))),
                  pl.BlockSpec((r, LANES), lambda bi, ni: (row(bi, ni), 0)),
                  pl.BlockSpec((r, LANES), lambda bi, ni: (row(bi, ni), 0)),
                  pl.BlockSpec((1, SWA_Q), const2), pl.BlockSpec((1, SWA_Q), const2),
                  pl.BlockSpec((1, SWA_KV), const2), pl.BlockSpec((1, SWA_KV), const2),
                  *[pl.BlockSpec(a.shape, const2) for a in consts]],
        out_specs=pl.BlockSpec((r, SWA_Q), lambda bi, ni: (row(bi, ni), 0)),
        out_shape=jax.ShapeDtypeStruct((m, SWA_Q), BF16),
        scratch_shapes=[pltpu.VMEM((2 * SWA_KV_HEADS, w, LANES), BF16),
                        pltpu.VMEM((2 * SWA_KV_HEADS, w, LANES), BF16)],
        compiler_params=_params("parallel", "arbitrary"),
        name="swa_core",
    )(sinks.astype(F32), qkv, qkv, cos_t, sin_t, qg, qgs, kg, kgs, *consts)


GDN_STACK = 4
GDN_ROWS = GDN_STACK * GDN_CHUNK
GDN_STEP_CHUNKS = 4


def _gdn_masks():
    i = np.arange(GDN_CHUNK)[:, None]
    j = np.arange(GDN_ROWS)[None, :] % GDN_CHUNK
    r = np.arange(GDN_ROWS)
    same = (r[:, None] // GDN_CHUNK) == (r[None, :] // GDN_CHUNK)
    return jnp.asarray(i >= j, F32), jnp.asarray(i > j, F32), jnp.asarray(same, BF16)


GDN_CONV_ROWS = 256
GDN_CONV_SUB = 64


def _gdn_shift_matrix():
    n = GDN_CONV_SUB
    sh = np.zeros((3 * n, 2 * n), np.float32)
    for d in range(1, GDN_CONV):
        sh[(d - 1) * n + np.arange(n), n + np.arange(n) - d] = 1.0
    return jnp.asarray(sh, BF16)


def _gdn_conv_kernel(prev_ref, cur_ref, cw_ref, shift_ref, ba_ref, alog_ref, dtb_ref, cum_ref,
                     o_ref, gate_ref, gct_ref):
    n = GDN_CONV_SUB
    dh = GDN_HEAD_DIM
    nh = GDN_V_HEADS
    wide = 2 * dh

    ba = ba_ref[...]
    beta = pltpu.roll(_sigmoid(ba), 2 * nh, 1)
    pre = ba + dtb_ref[...]
    softplus = jnp.maximum(pre, 0.0) + jnp.log1p(jnp.exp(-jnp.abs(pre)))
    g = -jnp.exp(alog_ref[...]) * softplus
    gc = jnp.dot(cum_ref[...], g, preferred_element_type=F32, precision=lax.Precision.HIGHEST)
    lane = lax.broadcasted_iota(jnp.int32, gc.shape, 1)
    gate_ref[...] = jnp.where(lane < 2 * nh, gc, beta)
    gct_ref[...] = gc.T

    keep = jnp.where(pl.program_id(1) > 0, 1.0, 0.0).astype(BF16)
    for j in range(GDN_CONV_DIM // wide):
        cs = slice(j * wide, (j + 1) * wide)
        w = [cw_ref[d:d + 1, cs] for d in range(GDN_CONV)]
        for t in range(GDN_CONV_ROWS // n):
            if t == 0:
                window = jnp.concatenate([prev_ref[:, cs] * keep, cur_ref[0:n, cs]], axis=0)
            else:
                window = cur_ref[(t - 1) * n:(t + 1) * n, cs]
            sh = jnp.dot(shift_ref[...], window, preferred_element_type=F32)
            acc = cur_ref[t * n:(t + 1) * n, cs].astype(F32) * w[GDN_CONV - 1]
            for d in range(1, GDN_CONV):
                acc = acc + sh[(d - 1) * n:d * n, :] * w[GDN_CONV - 1 - d]
            act = acc * _sigmoid(acc)
            if j * wide < 2 * GDN_QK:
                scale = dh ** -0.5 if j * wide < GDN_QK else 1.0
                act = jnp.concatenate(
                    [act[:, h * dh:(h + 1) * dh] * (scale * lax.rsqrt(
                        jnp.sum(act[:, h * dh:(h + 1) * dh] ** 2, axis=-1, keepdims=True) + NORM_EPS))
                     for h in range(wide // dh)], axis=1)
            o_ref[t * n:(t + 1) * n, cs] = act.astype(o_ref.dtype)


def _gdn_conv(qkvz, ba, conv_w, alog_row, dtb_row, b, s):
    r, n = GDN_CONV_ROWS, GDN_CONV_SUB
    nt = s // r
    m = b * s
    i = np.arange(r)
    cum = jnp.asarray((i[:, None] >= i[None, :]) & (i[:, None] // GDN_CHUNK == i[None, :] // GDN_CHUNK), F32)
    tile = lambda bi, ti: (bi * nt + ti, 0)
    const2 = lambda bi, ti: (0, 0)
    return pl.pallas_call(
        _gdn_conv_kernel,
        grid=(b, nt),
        in_specs=[pl.BlockSpec((n, GDN_CONV_DIM), lambda bi, ti: (jnp.maximum((bi * nt + ti) * (r // n) - 1, 0), 0)),
                  pl.BlockSpec((r, GDN_CONV_DIM), tile),
                  pl.BlockSpec((GDN_CONV, GDN_CONV_DIM), const2),
                  pl.BlockSpec((3 * n, 2 * n), const2),
                  pl.BlockSpec((r, LANES), tile),
                  pl.BlockSpec((1, LANES), const2),
                  pl.BlockSpec((1, LANES), const2),
                  pl.BlockSpec((r, r), const2)],
        out_specs=[pl.BlockSpec((r, GDN_CONV_DIM), tile),
                   pl.BlockSpec((r, LANES), tile),
                   pl.BlockSpec((LANES, r), lambda bi, ti: (0, bi * nt + ti))],
        out_shape=[jax.ShapeDtypeStruct((m, GDN_CONV_DIM), BF16),
                   jax.ShapeDtypeStruct((m, LANES), F32),
                   jax.ShapeDtypeStruct((LANES, m), F32)],
        compiler_params=_params("parallel", "parallel"),
        name="gdn_conv",
    )(qkvz, qkvz, conv_w, _gdn_shift_matrix(), ba, alog_row, dtb_row, cum)


def _gdn_kernel(qkv_ref, z_ref, gate_ref, gct_ref, og_ref, mt_ref, ms_ref, bd_ref, o_ref, s_ref):
    c = GDN_CHUNK
    dh = GDN_HEAD_DIM
    nh = GDN_V_HEADS

    @pl.when(pl.program_id(1) == 0)
    def _():
        s_ref[...] = jnp.zeros_like(s_ref)

    m_tril = mt_ref[...]
    m_strict = ms_ref[...]
    eye = m_tril - m_strict
    left = lax.broadcasted_iota(jnp.int32, (c, dh), 1) < c
    groups = range(nh // GDN_STACK)
    order = [[GDN_STACK * grp + r for r in (0, 2, 1, 3)] for grp in groups]
    steps = int(math.log2(c)) - 1
    dot = functools.partial(jnp.dot, preferred_element_type=F32)

    def block_diag(packed):
        return jnp.concatenate([packed] * GDN_STACK, axis=0) * bd_ref[...]

    def pack_pairs(x):
        tile = jnp.where(left, x[:c, :], x[c:, :])
        return jnp.concatenate([tile, tile], axis=1)

    def pack_cols(cols, grp):
        h0, h1, h2, h3 = order[grp]
        return jnp.concatenate([jnp.where(left, cols[h0], cols[h1]), jnp.where(left, cols[h2], cols[h3])], axis=1)

    def prepare(r0):
        tok = slice(r0, r0 + c)
        gates = gate_ref[tok, :]
        g_col = [jnp.broadcast_to(gates[:, nh + hv:nh + hv + 1], (c, dh)) for hv in range(nh)]
        b_col = [jnp.broadcast_to(gates[:, 2 * nh + hv:2 * nh + hv + 1], (c, dh)) for hv in range(nh)]
        eg = [jnp.exp(g_col[hv]) for hv in range(nh)]
        g_last = [g_col[hv][c - 1:c, :] for hv in range(nh)]
        qb = [qkv_ref[tok, hq * dh:(hq + 1) * dh] for hq in range(GDN_QK_HEADS)]
        kb = [qkv_ref[tok, GDN_QK + hq * dh:GDN_QK + (hq + 1) * dh] for hq in range(GDN_QK_HEADS)]
        q_h = [t.astype(F32) for t in qb]
        k_h = [t.astype(F32) for t in kb]

        low, attn = [], []
        for grp in groups:
            kab = jnp.concatenate([kb[2 * grp], kb[2 * grp + 1]], axis=0)
            qab = jnp.concatenate([qb[2 * grp], qb[2 * grp + 1]], axis=0)
            kk = pack_pairs(lax.dot_general(kab, kab, NT_DIMS, preferred_element_type=F32))
            qk = pack_pairs(lax.dot_general(qab, kab, NT_DIMS, preferred_element_type=F32))
            g_rows = jnp.concatenate([gct_ref[nh + hv:nh + hv + 1, tok] for hv in order[grp]], axis=1)
            decay = jnp.exp((pack_cols(g_col, grp) - g_rows) * m_tril) * m_tril
            low.append(kk * decay * pack_cols(b_col, grp) * m_strict)
            attn.append((qk * decay).astype(BF16))
        return dict(g_col=g_col, b_col=b_col, eg=eg, g_last=g_last, q_h=q_h, k_h=k_h, low=low, attn=attn)

    def inverse(low):
        idx = range(len(low))
        t_inv = [eye - low[i] for i in idx]
        pw = [low[i].astype(BF16) for i in idx]
        pw = [dot(pw[i], block_diag(pw[i])).astype(BF16) for i in idx]
        for it in range(steps):
            rhs = [block_diag(pw[i]) for i in idx]
            if it + 1 < steps:
                both = [dot(jnp.concatenate([pw[i], t_inv[i].astype(BF16)], axis=0), rhs[i]) for i in idx]
                pw = [both[i][:c].astype(BF16) for i in idx]
                t_inv = [t_inv[i] + both[i][c:] for i in idx]
            else:
                t_inv = [t_inv[i] + dot(t_inv[i].astype(BF16), rhs[i]) for i in idx]
        return t_inv

    def solve(r0, p, t_inv):
        tok = slice(r0, r0 + c)
        b_col, eg, k_h = p["b_col"], p["eg"], p["k_h"]
        uw = []
        for grp in groups:
            rhs = jnp.concatenate(
                [jnp.concatenate([qkv_ref[tok, 2 * GDN_QK + hv * dh:2 * GDN_QK + (hv + 1) * dh].astype(F32) * b_col[hv],
                                  k_h[hv // GDN_REP] * (b_col[hv] * eg[hv])], axis=1) for hv in order[grp]],
                axis=0).astype(BF16)
            uw.append(dot(block_diag(t_inv[grp].astype(BF16)), rhs))
        return uw

    def recur(r0, p, uw):
        tok = slice(r0, r0 + c)
        g_col, eg, g_last, q_h, k_h, attn = p["g_col"], p["eg"], p["g_last"], p["q_h"], p["k_h"], p["attn"]
        states = [s_ref[hv] for hv in range(nh)]
        sbs = [st.astype(BF16) for st in states]
        vnb, o_cross = [], {}
        for grp in groups:
            v_new = []
            for r, hv in enumerate(order[grp]):
                blk = slice(r * c, (r + 1) * c)
                q_in = (q_h[hv // GDN_REP] * eg[hv]).astype(BF16)
                on_state = dot(jnp.concatenate([uw[grp][blk, dh:].astype(BF16), q_in], axis=0), sbs[hv])
                v_new.append(uw[grp][blk, :dh] - on_state[:c])
                o_cross[hv] = on_state[c:]
            vnb.append(jnp.concatenate(v_new, axis=0).astype(BF16))
        o_intra = [dot(block_diag(attn[grp]), vnb[grp]) for grp in groups]
        for grp in groups:
            for r, hv in enumerate(order[grp]):
                blk = slice(r * c, (r + 1) * c)
                k_out = (k_h[hv // GDN_REP] * jnp.exp(g_last[hv] - g_col[hv])).astype(BF16)
                o = o_intra[grp][blk] + o_cross[hv]
                s_ref[hv] = states[hv] * jnp.exp(g_last[hv]) + lax.dot_general(
                    k_out, vnb[grp][blk], TN_DIMS, preferred_element_type=F32)
                zz = z_ref[tok, hv * dh:(hv + 1) * dh].astype(F32)
                o_ref[tok, hv * dh:(hv + 1) * dh] = (
                    _rms(o, og_ref[...]) * (zz * _sigmoid(zz))).astype(o_ref.dtype)

    prepared = [prepare(t * c) for t in range(GDN_STEP_CHUNKS)]
    t_inv = inverse([low for p in prepared for low in p["low"]])
    solved = [solve(t * c, p, t_inv[t * len(groups):(t + 1) * len(groups)]) for t, p in enumerate(prepared)]
    for t, p in enumerate(prepared):
        recur(t * c, p, solved[t])


def _gdn_core(qkv, qkvz, gates, gates_t, out_gain, b, s):
    c = GDN_STEP_CHUNKS * GDN_CHUNK
    n = s // c
    m = b * s
    m_tril, m_strict, block_sel = _gdn_masks()
    row = lambda bi, ni: bi * n + ni
    const2 = lambda bi, ni: (0, 0)
    return pl.pallas_call(
        _gdn_kernel,
        grid=(b, n),
        in_specs=[pl.BlockSpec((c, GDN_CONV_DIM), lambda bi, ni: (row(bi, ni), 0)),
                  pl.BlockSpec((c, GDN_V), lambda bi, ni: (row(bi, ni), GDN_CONV_DIM // GDN_V)),
                  pl.BlockSpec((c, LANES), lambda bi, ni: (row(bi, ni), 0)),
                  pl.BlockSpec((LANES, c), lambda bi, ni: (0, row(bi, ni))),
                  pl.BlockSpec((1, GDN_HEAD_DIM), const2),
                  pl.BlockSpec((GDN_CHUNK, GDN_ROWS), const2),
                  pl.BlockSpec((GDN_CHUNK, GDN_ROWS), const2),
                  pl.BlockSpec((GDN_ROWS, GDN_ROWS), const2)],
        out_specs=pl.BlockSpec((c, GDN_V), lambda bi, ni: (row(bi, ni), 0)),
        out_shape=jax.ShapeDtypeStruct((m, GDN_V), BF16),
        scratch_shapes=[pltpu.VMEM((GDN_V_HEADS, GDN_HEAD_DIM, GDN_HEAD_DIM), F32)],
        compiler_params=_params("parallel", "arbitrary"),
        name="gdn_core",
    )(qkv, qkvz, gates, gates_t, out_gain, m_tril, m_strict, block_sel)


def kernel(x, positions, norm_mix, norm_mlp, w_up, w_down, ret_w_in, ret_w_out, swa_w_in, swa_q_gain,
           swa_k_gain, swa_sinks, swa_w_out, gdn_w_in, gdn_conv_w, gdn_a_log, gdn_dt_bias, gdn_out_gain,
           gdn_w_out):
    b, s, d = x.shape
    m = b * s
    depth = norm_mix.shape[0]
    x2d = x.reshape(m, d)
    pos_col = positions.reshape(m, 1).astype(F32)
    ret_cos, ret_sin = _rope_tables(pos_col, RET_DK, RET_DK)
    swa_cos, swa_sin = _rope_tables(pos_col, SWA_HEAD_DIM, LANES)

    for i in range(depth):
        mixer, j = i % 3, i // 3
        gain = norm_mix[i][None, :]
        gate = dict()
        if mixer == 0:
            proj = _inproj(x2d, gain, ret_w_in[j].astype(BF16), tn=1024)
            y = _retention_core(proj, ret_cos, ret_sin, b, s)
            w_out = ret_w_out[j]
            gate_cols = (2 * RET_QK + RET_V) // RET_V
            gate = dict(gate=_ret_gate, gate_pieces=RET_HEADS, gate_args=(proj,),
                        gate_specs=lambda tm, idx: (pl.BlockSpec((tm, RET_V), lambda r: (idx(r), gate_cols)),))
        elif mixer == 1:
            proj = _inproj(x2d, gain, swa_w_in[j].astype(BF16), tn=SWA_Q + 2 * SWA_KV)
            y = _swa_core(proj, swa_cos, swa_sin, swa_q_gain[j], swa_k_gain[j], swa_sinks[j], b, s)
            w_out = swa_w_out[j]
        else:
            n_main = GDN_CONV_DIM + GDN_V
            w_main = gdn_w_in[j][:, :n_main].astype(BF16)
            w_gate = jnp.pad(gdn_w_in[j][:, n_main:], ((0, 0), (0, LANES - 2 * GDN_V_HEADS))).astype(BF16)
            proj, ba = _inproj(x2d, gain, w_main, tn=1024, w_aux=w_gate)
            lane_pad = (GDN_V_HEADS, LANES - 2 * GDN_V_HEADS)
            alog_row = jnp.pad(gdn_a_log[j].astype(F32), lane_pad)[None, :]
            dtb_row = jnp.pad(gdn_dt_bias[j].astype(F32), lane_pad)[None, :]
            qkv_act, gates, gates_t = _gdn_conv(proj, ba, gdn_conv_w[j].astype(F32), alog_row, dtb_row, b, s)
            y = _gdn_core(qkv_act, proj, gates, gates_t, gdn_out_gain[j].astype(F32)[None, :], b, s)
            w_out = gdn_w_out[j]
        x2d = _outproj_mlp(x2d, y, w_out.astype(BF16), norm_mlp[i][None, :],
                           w_up[i].astype(BF16), w_down[i].astype(BF16), **gate)
    return x2d.reshape(b, s, d)
```

```python
import functools
import math

import numpy as np
import jax
import jax.numpy as jnp
from jax import lax
from jax.experimental import pallas as pl
from jax.experimental.pallas import tpu as pltpu

F32 = jnp.float32
BF16 = jnp.bfloat16

NORM_EPS = 1e-6
ROPE_THETA = 10000.0

RET_HEADS = 4
RET_DK = 256
RET_DV = 512
RET_QK = RET_HEADS * RET_DK
RET_V = RET_HEADS * RET_DV
RET_CHUNK = 256
RET_STEP_CHUNKS = 2

SWA_HEADS = 16
SWA_KV_HEADS = 4
SWA_HEAD_DIM = 64
SWA_GROUP = SWA_HEADS // SWA_KV_HEADS
SWA_WINDOW = 128
SWA_STEP_BLOCKS = 4
SWA_Q = SWA_HEADS * SWA_HEAD_DIM
SWA_KV = SWA_KV_HEADS * SWA_HEAD_DIM

GDN_QK_HEADS = 8
GDN_V_HEADS = 16
GDN_HEAD_DIM = 128
GDN_QK = GDN_QK_HEADS * GDN_HEAD_DIM
GDN_V = GDN_V_HEADS * GDN_HEAD_DIM
GDN_CONV_DIM = 2 * GDN_QK + GDN_V
GDN_CONV = 4
GDN_CHUNK = 64
GDN_REP = GDN_V_HEADS // GDN_QK_HEADS

LANES = 128
VMEM_LIMIT = 56 * 1024 * 1024
PROJ_ROWS = 512

LOG2_E = math.log2(math.e)

NT_DIMS = (((1,), (1,)), ((), ()))
TN_DIMS = (((0,), (0,)), ((), ()))


def _params(*sem):
    return pltpu.CompilerParams(dimension_semantics=sem, vmem_limit_bytes=VMEM_LIMIT)


def _sigmoid(x):
    return 1.0 / (1.0 + jnp.exp2(x * -LOG2_E))


def _rms(x, gain_row):
    ms = jnp.mean(x * x, axis=-1, keepdims=True)
    return x * lax.rsqrt(ms + NORM_EPS) * gain_row


def _rope_table_kernel(pos_ref, inv_ref, sign_ref, cos_ref, sin_ref):
    ang = pos_ref[...] * inv_ref[...]
    fold = cos_ref.shape[1] // ang.shape[1]
    cos_ref[...] = jnp.concatenate([jnp.cos(ang)] * fold, axis=1)
    sin_ref[...] = jnp.concatenate([jnp.sin(ang)] * fold, axis=1) * sign_ref[...]


def _rope_tables(pos_col, head_dim, width):
    m = pos_col.shape[0]
    half = head_dim // 2
    inv = ROPE_THETA ** (-np.arange(0, head_dim, 2, dtype=np.float32) / head_dim)
    reps = width // head_dim
    inv_row = np.tile(np.concatenate([inv, inv]), reps)[None, :].astype(np.float32)
    sign_row = np.tile(np.concatenate([-np.ones(half), np.ones(half)]), reps)[None, :].astype(np.float32)
    uniq = max(half, LANES) if half % LANES == 0 else width
    inv_row = inv_row[:, :uniq]
    tm = min(m, 1024)
    return pl.pallas_call(
        _rope_table_kernel,
        grid=(m // tm,),
        in_specs=[pl.BlockSpec((tm, 1), lambda i: (i, 0)),
                  pl.BlockSpec((1, uniq), lambda i: (0, 0)),
                  pl.BlockSpec((1, width), lambda i: (0, 0))],
        out_specs=[pl.BlockSpec((tm, width), lambda i: (i, 0)),
                   pl.BlockSpec((tm, width), lambda i: (i, 0))],
        out_shape=[jax.ShapeDtypeStruct((m, width), F32)] * 2,
        compiler_params=_params("parallel"),
        name="rope_tables",
    )(pos_col, jnp.asarray(inv_row), jnp.asarray(sign_row))


def _resident(shape):
    return pl.BlockSpec(shape, lambda i: (0,) * len(shape), pipeline_mode=pl.Buffered(1))


def _inproj_kernel(tn, x_ref, g_ref, w_ref, o_ref):
    h = _rms(x_ref[...], g_ref[...]).astype(BF16)
    for j in range(o_ref.shape[1] // tn):
        cols = slice(j * tn, (j + 1) * tn)
        o_ref[:, cols] = jnp.dot(h, w_ref[:, cols], preferred_element_type=F32).astype(o_ref.dtype)


def _inproj_aux_kernel(tn, x_ref, g_ref, w_ref, wa_ref, o_ref, oa_ref):
    h = _rms(x_ref[...], g_ref[...]).astype(BF16)
    oa_ref[...] = jnp.dot(h, wa_ref[...], preferred_element_type=F32)
    for j in range(o_ref.shape[1] // tn):
        cols = slice(j * tn, (j + 1) * tn)
        o_ref[:, cols] = jnp.dot(h, w_ref[:, cols], preferred_element_type=F32).astype(o_ref.dtype)


def _inproj(x2d, gain, w, tn, w_aux=None):
    m, d = x2d.shape
    n = w.shape[1]
    tm = min(m, PROJ_ROWS)
    x_spec = pl.BlockSpec((tm, d), lambda i: (i, 0))
    o_spec = pl.BlockSpec((tm, n), lambda i: (i, 0))
    if w_aux is None:
        return pl.pallas_call(
            functools.partial(_inproj_kernel, tn), grid=(m // tm,),
            in_specs=[x_spec, _resident((1, d)), _resident((d, n))], out_specs=o_spec,
            out_shape=jax.ShapeDtypeStruct((m, n), BF16),
            compiler_params=_params("parallel"),
            name="inproj",
        )(x2d, gain, w)
    na = w_aux.shape[1]
    return pl.pallas_call(
        functools.partial(_inproj_aux_kernel, tn), grid=(m // tm,),
        in_specs=[x_spec, _resident((1, d)), _resident((d, n)), _resident((d, na))],
        out_specs=[o_spec, pl.BlockSpec((tm, na), lambda i: (i, 0))],
        out_shape=[jax.ShapeDtypeStruct((m, n), BF16), jax.ShapeDtypeStruct((m, na), F32)],
        compiler_params=_params("parallel"),
        name="inproj_aux",
    )(x2d, gain, w, w_aux)


def _ret_gate(p, o_ref, g_ref):
    cols = slice(p * RET_DV, (p + 1) * RET_DV)
    o = o_ref[:, cols].astype(F32)
    mu = jnp.mean(o, axis=-1, keepdims=True)
    oc = o - mu
    var = jnp.mean(oc * oc, axis=-1, keepdims=True)
    g = g_ref[:, cols].astype(F32)
    return cols, (oc * lax.rsqrt(var + NORM_EPS) * (g * _sigmoid(g))).astype(BF16)


def _mlp_chunks(tf, h, wu_ref, wd_ref, o_ref, after_chunk=None):
    for j in range(wu_ref.shape[1] // tf):
        a = jnp.dot(h, wu_ref[:, j * tf:(j + 1) * tf], preferred_element_type=F32)
        a = jnp.maximum(a, 0.0)
        a = (a * a).astype(BF16)
        o_ref[...] += jnp.dot(a, wd_ref[j * tf:(j + 1) * tf, :], preferred_element_type=F32)
        if after_chunk is not None:
            after_chunk(j)


def _outproj_mlp_kernel(tf, x_ref, y_ref, wo_ref, g_ref, wu_ref, wd_ref, o_ref):
    o_ref[...] = x_ref[...] + jnp.dot(y_ref[...], wo_ref[...], preferred_element_type=F32)
    _mlp_chunks(tf, _rms(o_ref[...], g_ref[...]).astype(BF16), wu_ref, wd_ref, o_ref)


def _gated_outproj_mlp_kernel(tf, gate, n_pieces, n_gate, x_ref, *refs):
    gate_refs, (wo_ref, g_ref, wu_ref, wd_ref, o_ref, x1_buf) = refs[:n_gate], refs[n_gate:]

    @pl.when(pl.program_id(0) == 0)
    def _():
        x1_buf[...] = jnp.zeros_like(x1_buf)

    x1 = x1_buf[...]
    o_ref[...] = x1
    every = (wu_ref.shape[1] // tf) // n_pieces

    def outproj_piece(j):
        if (j + 1) % every:
            return
        p = j // every
        cols, y = gate(p, *gate_refs)
        part = jnp.dot(y, wo_ref[cols, :], preferred_element_type=F32)
        x1_buf[...] = (x_ref[...] if p == 0 else x1_buf[...]) + part

    _mlp_chunks(tf, _rms(x1, g_ref[...]).astype(BF16), wu_ref, wd_ref, o_ref, outproj_piece)


def _outproj_mlp(x2d, y, w_out, gain, w_up, w_down, gate=None, gate_pieces=1, gate_args=(), gate_specs=(),
                 tf=512):
    m, d = x2d.shape
    ky = y.shape[1]
    ff = w_up.shape[1]
    tm = min(m, PROJ_ROWS)
    nt = m // tm
    weights = [_resident((ky, d)), _resident((1, d)), _resident((d, ff)), _resident((ff, d))]
    if gate is None:
        return pl.pallas_call(
            functools.partial(_outproj_mlp_kernel, tf),
            grid=(nt,),
            in_specs=[pl.BlockSpec((tm, d), lambda i: (i, 0)), pl.BlockSpec((tm, ky), lambda i: (i, 0)), *weights],
            out_specs=pl.BlockSpec((tm, d), lambda i: (i, 0)),
            out_shape=jax.ShapeDtypeStruct((m, d), F32),
            compiler_params=_params("parallel"),
            name="outproj_mlp",
        )(x2d, y, w_out, gain, w_up, w_down)
    done = lambda i: jnp.maximum(i - 1, 0)
    ahead = lambda i: jnp.minimum(i, nt - 1)
    return pl.pallas_call(
        functools.partial(_gated_outproj_mlp_kernel, tf, gate, gate_pieces, 1 + len(gate_args)),
        grid=(nt + 1,),
        in_specs=[pl.BlockSpec((tm, d), lambda i: (ahead(i), 0)),
                  pl.BlockSpec((tm, ky), lambda i: (ahead(i), 0)),
                  *gate_specs(tm, ahead), *weights],
        out_specs=pl.BlockSpec((tm, d), lambda i: (done(i), 0)),
        out_shape=jax.ShapeDtypeStruct((m, d), F32),
        scratch_shapes=[pltpu.VMEM((tm, d), F32)],
        compiler_params=_params("arbitrary"),
        name="gated_outproj_mlp",
    )(x2d, y, *gate_args, w_out, gain, w_up, w_down)


def _ret_constants():
    c = RET_CHUNK
    idx = np.arange(c, dtype=np.float64)
    log_gamma = np.log1p(-(2.0 ** (-5.0 - np.arange(RET_HEADS, dtype=np.float64))))
    diff = idx[:, None] - idx[None, :]
    scale = RET_DK ** -0.5
    dmat = np.where(diff >= 0, np.exp(log_gamma[:, None, None] * np.maximum(diff, 0.0)), 0.0) * scale
    qd = np.exp(log_gamma[:, None] * (idx + 1.0))
    kd = np.exp(log_gamma[:, None] * (c - 1.0 - idx)) * scale
    qd = np.broadcast_to(qd[:, :, None], (RET_HEADS, c, RET_DK))
    kd = np.broadcast_to(kd[:, :, None], (RET_HEADS, c, RET_DK))
    chunk_decay = [float(v) for v in np.exp(log_gamma * c)]
    return (jnp.asarray(dmat, F32), jnp.asarray(qd, F32), jnp.asarray(kd, F32), chunk_decay)


def _ret_kernel(chunk_decay, q_ref, k_ref, v_ref, cos_ref, sin_ref, dmat_ref, qd_ref, kd_ref,
                o_ref, s_ref):
    @pl.when(pl.program_id(1) == 0)
    def _():
        s_ref[...] = jnp.zeros_like(s_ref)

    c = RET_CHUNK
    half = RET_DK // 2
    heads = range(RET_HEADS)
    dot = functools.partial(jnp.dot, preferred_element_type=F32)

    qr, kr, v, intra = {}, {}, {}, {}
    for t in range(RET_STEP_CHUNKS):
        tok = slice(t * c, (t + 1) * c)
        cos = cos_ref[tok, :]
        sin = sin_ref[tok, :]

        def rope(x):
            return x * cos + jnp.concatenate([x[:, half:], x[:, :half]], axis=1) * sin

        for h in heads:
            qr[t, h] = rope(q_ref[tok, h * RET_DK:(h + 1) * RET_DK].astype(F32))
            kr[t, h] = rope(k_ref[tok, h * RET_DK:(h + 1) * RET_DK].astype(F32))
            v[t, h] = v_ref[tok, h * RET_DV:(h + 1) * RET_DV]
    for key in qr:
        scores = lax.dot_general(qr[key].astype(BF16), kr[key].astype(BF16), NT_DIMS,
                                 preferred_element_type=F32) * dmat_ref[key[1]]
        intra[key] = dot(scores.astype(BF16), v[key])
    for t in range(RET_STEP_CHUNKS):
        tok = slice(t * c, (t + 1) * c)
        state = [s_ref[h] for h in heads]
        o = [intra[t, h] + dot((qr[t, h] * qd_ref[h]).astype(BF16), state[h].astype(BF16)) for h in heads]
        for h in heads:
            kdec = (kr[t, h] * kd_ref[h]).astype(BF16)
            s_ref[h] = state[h] * chunk_decay[h] + lax.dot_general(kdec, v[t, h], TN_DIMS,
                                                                   preferred_element_type=F32)
        for h in heads:
            o_ref[tok, h * RET_DV:(h + 1) * RET_DV] = o[h].astype(o_ref.dtype)


def _retention_core(qkvg, cos_t, sin_t, b, s):
    c = RET_STEP_CHUNKS * RET_CHUNK
    n = s // c
    m = b * s
    dmat, qd, kd, chunk_decay = _ret_constants()
    row = lambda bi, ni: bi * n + ni
    const3 = lambda bi, ni: (0, 0, 0)
    return pl.pallas_call(
        functools.partial(_ret_kernel, chunk_decay),
        grid=(b, n),
        in_specs=[pl.BlockSpec((c, RET_QK), lambda bi, ni: (row(bi, ni), 0)),
                  pl.BlockSpec((c, RET_QK), lambda bi, ni: (row(bi, ni), 1)),
                  pl.BlockSpec((c, RET_V), lambda bi, ni: (row(bi, ni), 1)),
                  pl.BlockSpec((c, RET_DK), lambda bi, ni: (row(bi, ni), 0)),
                  pl.BlockSpec((c, RET_DK), lambda bi, ni: (row(bi, ni), 0)),
                  pl.BlockSpec((RET_HEADS, RET_CHUNK, RET_CHUNK), const3),
                  pl.BlockSpec((RET_HEADS, RET_CHUNK, RET_DK), const3),
                  pl.BlockSpec((RET_HEADS, RET_CHUNK, RET_DK), const3)],
        out_specs=pl.BlockSpec((c, RET_V), lambda bi, ni: (row(bi, ni), 0)),
        out_shape=jax.ShapeDtypeStruct((m, RET_V), BF16),
        scratch_shapes=[pltpu.VMEM((RET_HEADS, RET_DK, RET_DV), F32)],
        compiler_params=_params("parallel", "arbitrary"),
        name="retention_core",
    )(qkvg, qkvg, qkvg, cos_t, sin_t, dmat, qd, kd)


def _swa_constants():
    dh, half = SWA_HEAD_DIM, SWA_HEAD_DIM // 2
    i = np.arange(2 * LANES)
    swap = (i[:, None] == ((i[None, :] // dh) * dh + (i[None, :] % dh + half) % dh))
    seg_q = (np.arange(SWA_Q)[:, None] // dh) == np.arange(LANES)[None, :]
    seg_k = (np.arange(SWA_KV)[:, None] // dh) == np.arange(LANES)[None, :]
    bf = lambda a: jnp.asarray(a, BF16)
    return bf(swap), bf(seg_q), bf(seg_k), bf(seg_q.T * (dh ** -0.5)), bf(seg_k.T)


def _swa_kernel(sink_ref, q_ref, kv_ref, cos_ref, sin_ref, qg_ref, qgs_ref, kg_ref, kgs_ref,
                swap_ref, segq_ref, segk_ref, expq_ref, expk_ref, o_ref, k_prev, v_prev):
    w = SWA_WINDOW
    ni = pl.program_id(1)
    dot = functools.partial(jnp.dot, preferred_element_type=F32)

    @pl.when(ni == 0)
    def _():
        k_prev[...] = jnp.zeros_like(k_prev)
        v_prev[...] = jnp.zeros_like(v_prev)

    cos = cos_ref[...]
    sin = sin_ref[...]

    def normed_rope(xb, gain_ref, gain_swapped_ref, seg_ref, exp_ref):
        width = xb.shape[1]
        reps = width // LANES
        xf = xb.astype(F32)
        partner = jnp.concatenate(
            [dot(xb[:, t * 2 * LANES:(t + 1) * 2 * LANES], swap_ref[...]) for t in range(reps // 2)], axis=1)
        rot = (xf * (jnp.concatenate([cos] * reps, axis=1) * gain_ref[...])
               + partner * (jnp.concatenate([sin] * reps, axis=1) * gain_swapped_ref[...]))
        ss = dot((xf * xf).astype(BF16), seg_ref[...])
        inv = lax.rsqrt(ss * (1.0 / SWA_HEAD_DIM) + NORM_EPS)
        inv_hi = inv.astype(BF16)
        inv_lo = (inv - inv_hi.astype(F32)).astype(BF16)
        return rot * (dot(inv_hi, exp_ref[...]) + dot(inv_lo, exp_ref[...]))

    q_n = normed_rope(q_ref[...], qg_ref, qgs_ref, segq_ref, expq_ref).astype(BF16)
    k_n = normed_rope(kv_ref[:, :SWA_KV], kg_ref, kgs_ref, segk_ref, expk_ref)
    v_f = kv_ref[:, SWA_KV:].astype(F32)

    left = lax.broadcasted_iota(jnp.int32, (SWA_STEP_BLOCKS * w, LANES), 1) < SWA_HEAD_DIM

    def placements(x):
        out = []
        for kh in range(SWA_KV_HEADS):
            tile = x[:, (kh // 2) * LANES:(kh // 2 + 1) * LANES]
            moved = pltpu.roll(tile, SWA_HEAD_DIM, 1)
            in_left, in_right = (tile, moved) if kh % 2 == 0 else (moved, tile)
            out.append([jnp.where(left, in_left, 0.0).astype(BF16), jnp.where(left, 0.0, in_right).astype(BF16)])
        return out

    k_cur = placements(k_n)
    v_cur = placements(v_f)

    key = lax.broadcasted_iota(jnp.int32, (2 * w, 2 * w), 0)
    qry = lax.broadcasted_iota(jnp.int32, (2 * w, 2 * w), 1) % w
    rel = qry + w - key
    band = (rel >= 0) & (rel < w)
    first_key = jnp.where(ni > 0, 0, w)
    valid = [band & (key >= first_key)] + [band] * (SWA_STEP_BLOCKS - 1)

    def with_prev(cur, prev_ref, slot, blk):
        if blk == 0:
            return jnp.concatenate([prev_ref[slot], cur[:w]], axis=0)
        return cur[(blk - 1) * w:(blk + 1) * w]

    pairs = [(blk, kh, sd) for blk in range(SWA_STEP_BLOCKS) for kh in range(SWA_KV_HEADS) for sd in range(2)]
    sc, sink = {}, {}
    for blk, kh, sd in pairs:
        k2 = with_prev(k_cur[kh][sd], k_prev, 2 * kh + sd, blk)
        q2 = jnp.concatenate([q_n[blk * w:(blk + 1) * w, (2 * kh + t) * LANES:(2 * kh + t + 1) * LANES]
                              for t in range(2)], axis=0)
        sc[blk, kh, sd] = lax.dot_general(k2, q2, NT_DIMS, preferred_element_type=F32)
        heads = [SWA_GROUP * kh + 2 * t + sd for t in range(2)]
        sink[blk, kh, sd] = jnp.concatenate([jnp.full((1, w), sink_ref[h], F32) for h in heads], axis=1)
    sc = {pr: jnp.where(valid[pr[0]], sc[pr], -jnp.inf) for pr in pairs}
    mx = {pr: jnp.maximum(jnp.max(sc[pr], axis=0, keepdims=True), sink[pr]) for pr in pairs}
    p = {pr: jnp.exp(sc[pr] - mx[pr]) for pr in pairs}
    den = {pr: jnp.sum(p[pr], axis=0, keepdims=True) + jnp.exp(sink[pr] - mx[pr]) for pr in pairs}
    p = {pr: (p[pr] * (1.0 / den[pr])).astype(BF16) for pr in pairs}
    for blk in range(SWA_STEP_BLOCKS):
        for kh in range(SWA_KV_HEADS):
            v2 = jnp.concatenate([with_prev(v_cur[kh][sd], v_prev, 2 * kh + sd, blk) for sd in range(2)], axis=0)
            p2 = jnp.concatenate([p[blk, kh, 0], p[blk, kh, 1]], axis=0)
            o_t = lax.dot_general(v2, p2, TN_DIMS, preferred_element_type=F32)
            for t in range(2):
                o_ref[blk * w:(blk + 1) * w, (2 * kh + t) * LANES:(2 * kh + t + 1) * LANES] = (
                    o_t[:, t * w:(t + 1) * w].T.astype(o_ref.dtype))
    last = slice((SWA_STEP_BLOCKS - 1) * w, SWA_STEP_BLOCKS * w)
    for kh in range(SWA_KV_HEADS):
        for sd in range(2):
            k_prev[2 * kh + sd] = k_cur[kh][sd][last]
            v_prev[2 * kh + sd] = v_cur[kh][sd][last]


def _swa_core(qkv, cos_t, sin_t, q_gain, k_gain, sinks, b, s):
    w = SWA_WINDOW
    r = SWA_STEP_BLOCKS * w
    n = s // r
    m = b * s
    half = SWA_HEAD_DIM // 2
    swapped = lambda g: jnp.concatenate([g[half:], g[:half]])
    qg = jnp.tile(q_gain.astype(F32), SWA_HEADS)[None, :]
    qgs = jnp.tile(swapped(q_gain.astype(F32)), SWA_HEADS)[None, :]
    kg = jnp.tile(k_gain.astype(F32), SWA_KV_HEADS)[None, :]
    kgs = jnp.tile(swapped(k_gain.astype(F32)), SWA_KV_HEADS)[None, :]
    consts = _swa_constants()
    row = lambda bi, ni: bi * n + ni
    const2 = lambda bi, ni: (0, 0)
    return pl.pallas_call(
        _swa_kernel,
        grid=(b, n),
        in_specs=[pl.BlockSpec(memory_space=pltpu.SMEM),
                  pl.BlockSpec((r, SWA_Q), lambda bi, ni: (row(bi, ni), 0)),
                  pl.BlockSpec((r, 2 * SWA_KV), lambda bi, ni: (row(bi, ni), SWA_Q // (2 * SWA_KV))),
                  pl.BlockSpec((r, LANES), lambda bi, ni: (row(bi, ni), 0)),
                  pl.BlockSpec((r, LANES), lambda bi, ni: (row(bi, ni), 0)),
                  pl.BlockSpec((1, SWA_Q), const2), pl.BlockSpec((1, SWA_Q), const2),
                  pl.BlockSpec((1, SWA_KV), const2), pl.BlockSpec((1, SWA_KV), const2),
                  *[pl.BlockSpec(a.shape, const2) for a in consts]],
        out_specs=pl.BlockSpec((r, SWA_Q), lambda bi, ni: (row(bi, ni), 0)),
        out_shape=jax.ShapeDtypeStruct((m, SWA_Q), BF16),
        scratch_shapes=[pltpu.VMEM((2 * SWA_KV_HEADS, w, LANES), BF16),
                        pltpu.VMEM((2 * SWA_KV_HEADS, w, LANES), BF16)],
        compiler_params=_params("parallel", "arbitrary"),
        name="swa_core",
    )(sinks.astype(F32), qkv, qkv, cos_t, sin_t, qg, qgs, kg, kgs, *consts)


GDN_STACK = 4
GDN_ROWS = GDN_STACK * GDN_CHUNK
GDN_STEP_CHUNKS = 4


def _gdn_masks():
    i = np.arange(GDN_CHUNK)[:, None]
    j = np.arange(GDN_ROWS)[None, :] % GDN_CHUNK
    r = np.arange(GDN_ROWS)
    same = (r[:, None] // GDN_CHUNK) == (r[None, :] // GDN_CHUNK)
    return jnp.asarray(i >= j, F32), jnp.asarray(i > j, F32), jnp.asarray(same, BF16)


GDN_CONV_ROWS = 256
GDN_CONV_SUB = 64


def _gdn_shift_matrix():
    n = GDN_CONV_SUB
    sh = np.zeros((3 * n, 2 * n), np.float32)
    for d in range(1, GDN_CONV):
        sh[(d - 1) * n + np.arange(n), n + np.arange(n) - d] = 1.0
    return jnp.asarray(sh, BF16)


def _gdn_conv_kernel(prev_ref, cur_ref, cw_ref, shift_ref, ba_ref, alog_ref, dtb_ref, cum_ref,
                     o_ref, gate_ref, gct_ref):
    n = GDN_CONV_SUB
    dh = GDN_HEAD_DIM
    nh = GDN_V_HEADS
    wide = 2 * dh

    ba = ba_ref[...]
    beta = pltpu.roll(_sigmoid(ba), 2 * nh, 1)
    pre = ba + dtb_ref[...]
    softplus = jnp.maximum(pre, 0.0) + jnp.log1p(jnp.exp(-jnp.abs(pre)))
    g = -jnp.exp(alog_ref[...]) * softplus
    gc = jnp.dot(cum_ref[...], g, preferred_element_type=F32, precision=lax.Precision.HIGHEST)
    lane = lax.broadcasted_iota(jnp.int32, gc.shape, 1)
    gate_ref[...] = jnp.where(lane < 2 * nh, gc, beta)
    gct_ref[...] = gc.T

    keep = jnp.where(pl.program_id(1) > 0, 1.0, 0.0).astype(BF16)
    for j in range(GDN_CONV_DIM // wide):
        cs = slice(j * wide, (j + 1) * wide)
        w = [cw_ref[d:d + 1, cs] for d in range(GDN_CONV)]
        for t in range(GDN_CONV_ROWS // n):
            if t == 0:
                window = jnp.concatenate([prev_ref[:, cs] * keep, cur_ref[0:n, cs]], axis=0)
            else:
                window = cur_ref[(t - 1) * n:(t + 1) * n, cs]
            sh = jnp.dot(shift_ref[...], window, preferred_element_type=F32)
            acc = cur_ref[t * n:(t + 1) * n, cs].astype(F32) * w[GDN_CONV - 1]
            for d in range(1, GDN_CONV):
                acc = acc + sh[(d - 1) * n:d * n, :] * w[GDN_CONV - 1 - d]
            act = acc * _sigmoid(acc)
            if j * wide < 2 * GDN_QK:
                scale = dh ** -0.5 if j * wide < GDN_QK else 1.0
                act = jnp.concatenate(
                    [act[:, h * dh:(h + 1) * dh] * (scale * lax.rsqrt(
                        jnp.sum(act[:, h * dh:(h + 1) * dh] ** 2, axis=-1, keepdims=True) + NORM_EPS))
                     for h in range(wide // dh)], axis=1)
            o_ref[t * n:(t + 1) * n, cs] = act.astype(o_ref.dtype)


def _gdn_conv(qkvz, ba, conv_w, alog_row, dtb_row, b, s):
    r, n = GDN_CONV_ROWS, GDN_CONV_SUB
    nt = s // r
    m = b * s
    i = np.arange(r)
    cum = jnp.asarray((i[:, None] >= i[None, :]) & (i[:, None] // GDN_CHUNK == i[None, :] // GDN_CHUNK), F32)
    tile = lambda bi, ti: (bi * nt + ti, 0)
    const2 = lambda bi, ti: (0, 0)
    return pl.pallas_call(
        _gdn_conv_kernel,
        grid=(b, nt),
        in_specs=[pl.BlockSpec((n, GDN_CONV_DIM), lambda bi, ti: (jnp.maximum((bi * nt + ti) * (r // n) - 1, 0), 0)),
                  pl.BlockSpec((r, GDN_CONV_DIM), tile),
                  pl.BlockSpec((GDN_CONV, GDN_CONV_DIM), const2),
                  pl.BlockSpec((3 * n, 2 * n), const2),
                  pl.BlockSpec((r, LANES), tile),
                  pl.BlockSpec((1, LANES), const2),
                  pl.BlockSpec((1, LANES), const2),
                  pl.BlockSpec((r, r), const2)],
        out_specs=[pl.BlockSpec((r, GDN_CONV_DIM), tile),
                   pl.BlockSpec((r, LANES), tile),
                   pl.BlockSpec((LANES, r), lambda bi, ti: (0, bi * nt + ti))],
        out_shape=[jax.ShapeDtypeStruct((m, GDN_CONV_DIM), BF16),
                   jax.ShapeDtypeStruct((m, LANES), F32),
                   jax.ShapeDtypeStruct((LANES, m), F32)],
        compiler_params=_params("parallel", "parallel"),
        name="gdn_conv",
    )(qkvz, qkvz, conv_w, _gdn_shift_matrix(), ba, alog_row, dtb_row, cum)


def _gdn_kernel(qkv_ref, z_ref, gate_ref, gct_ref, og_ref, mt_ref, ms_ref, bd_ref, o_ref, s_ref):
    c = GDN_CHUNK
    dh = GDN_HEAD_DIM
    nh = GDN_V_HEADS

    @pl.when(pl.program_id(1) == 0)
    def _():
        s_ref[...] = jnp.zeros_like(s_ref)

    m_tril = mt_ref[...]
    m_strict = ms_ref[...]
    eye = m_tril - m_strict
    left = lax.broadcasted_iota(jnp.int32, (c, dh), 1) < c
    groups = range(nh // GDN_STACK)
    order = [[GDN_STACK * grp + r for r in (0, 2, 1, 3)] for grp in groups]
    steps = int(math.log2(c)) - 1
    dot = functools.partial(jnp.dot, preferred_element_type=F32)

    def block_diag(packed):
        return jnp.concatenate([packed] * GDN_STACK, axis=0) * bd_ref[...]

    def pack_pairs(x):
        tile = jnp.where(left, x[:c, :], x[c:, :])
        return jnp.concatenate([tile, tile], axis=1)

    def pack_cols(cols, grp):
        h0, h1, h2, h3 = order[grp]
        return jnp.concatenate([jnp.where(left, cols[h0], cols[h1]), jnp.where(left, cols[h2], cols[h3])], axis=1)

    def prepare(r0):
        tok = slice(r0, r0 + c)
        gates = gate_ref[tok, :]
        g_col = [jnp.broadcast_to(gates[:, nh + hv:nh + hv + 1], (c, dh)) for hv in range(nh)]
        b_col = [jnp.broadcast_to(gates[:, 2 * nh + hv:2 * nh + hv + 1], (c, dh)) for hv in range(nh)]
        eg = [jnp.exp(g_col[hv]) for hv in range(nh)]
        g_last = [g_col[hv][c - 1:c, :] for hv in range(nh)]
        qb = [qkv_ref[tok, hq * dh:(hq + 1) * dh] for hq in range(GDN_QK_HEADS)]
        kb = [qkv_ref[tok, GDN_QK + hq * dh:GDN_QK + (hq + 1) * dh] for hq in range(GDN_QK_HEADS)]
        q_h = [t.astype(F32) for t in qb]
        k_h = [t.astype(F32) for t in kb]

        low, attn = [], []
        for grp in groups:
            kab = jnp.concatenate([kb[2 * grp], kb[2 * grp + 1]], axis=0)
            qab = jnp.concatenate([qb[2 * grp], qb[2 * grp + 1]], axis=0)
            kk = pack_pairs(lax.dot_general(kab, kab, NT_DIMS, preferred_element_type=F32))
            qk = pack_pairs(lax.dot_general(qab, kab, NT_DIMS, preferred_element_type=F32))
            g_rows = jnp.concatenate([gct_ref[nh + hv:nh + hv + 1, tok] for hv in order[grp]], axis=1)
            decay = jnp.exp((pack_cols(g_col, grp) - g_rows) * m_tril) * m_tril
            low.append(kk * decay * pack_cols(b_col, grp) * m_strict)
            attn.append((qk * decay).astype(BF16))
        return dict(g_col=g_col, b_col=b_col, eg=eg, g_last=g_last, q_h=q_h, k_h=k_h, low=low, attn=attn)

    def inverse(low, out):
        idx = range(len(low))
        t_inv = [eye - low[i] for i in idx]
        pw = [low[i].astype(BF16) for i in idx]
        pw = [dot(pw[i], block_diag(pw[i])).astype(BF16) for i in idx]
        yield None
        for it in range(steps):
            rhs = [block_diag(pw[i]) for i in idx]
            if it + 1 < steps:
                both = [dot(jnp.concatenate([pw[i], t_inv[i].astype(BF16)], axis=0), rhs[i]) for i in idx]
                pw = [both[i][:c].astype(BF16) for i in idx]
                t_inv = [t_inv[i] + both[i][c:] for i in idx]
                yield None
            else:
                out.extend(t_inv[i] + dot(t_inv[i].astype(BF16), rhs[i]) for i in idx)

    def solve(r0, p, t_inv):
        tok = slice(r0, r0 + c)
        b_col, eg, k_h = p["b_col"], p["eg"], p["k_h"]
        uw = []
        for grp in groups:
            rhs = jnp.concatenate(
                [jnp.concatenate([qkv_ref[tok, 2 * GDN_QK + hv * dh:2 * GDN_QK + (hv + 1) * dh].astype(F32) * b_col[hv],
                                  k_h[hv // GDN_REP] * (b_col[hv] * eg[hv])], axis=1) for hv in order[grp]],
                axis=0).astype(BF16)
            uw.append(dot(block_diag(t_inv[grp].astype(BF16)), rhs))
        return uw

    def recur(r0, p, uw, between=lambda: None):
        tok = slice(r0, r0 + c)
        g_col, eg, g_last, q_h, k_h, attn = p["g_col"], p["eg"], p["g_last"], p["q_h"], p["k_h"], p["attn"]
        states = [s_ref[hv] for hv in range(nh)]
        sbs = [st.astype(BF16) for st in states]
        vnb, o_cross = [], {}
        for grp in groups:
            v_new = []
            for r, hv in enumerate(order[grp]):
                blk = slice(r * c, (r + 1) * c)
                q_in = (q_h[hv // GDN_REP] * eg[hv]).astype(BF16)
                on_state = dot(jnp.concatenate([uw[grp][blk, dh:].astype(BF16), q_in], axis=0), sbs[hv])
                v_new.append(uw[grp][blk, :dh] - on_state[:c])
                o_cross[hv] = on_state[c:]
            vnb.append(jnp.concatenate(v_new, axis=0).astype(BF16))
        between()
        o_intra = [dot(block_diag(attn[grp]), vnb[grp]) for grp in groups]
        between()
        for grp in groups:
            for r, hv in enumerate(order[grp]):
                blk = slice(r * c, (r + 1) * c)
                k_out = (k_h[hv // GDN_REP] * jnp.exp(g_last[hv] - g_col[hv])).astype(BF16)
                o = o_intra[grp][blk] + o_cross[hv]
                s_ref[hv] = states[hv] * jnp.exp(g_last[hv]) + lax.dot_general(
                    k_out, vnb[grp][blk], TN_DIMS, preferred_element_type=F32)
                zz = z_ref[tok, hv * dh:(hv + 1) * dh].astype(F32)
                o_ref[tok, hv * dh:(hv + 1) * dh] = (
                    _rms(o, og_ref[...]) * (zz * _sigmoid(zz))).astype(o_ref.dtype)
        between()

    n_grp = len(groups)
    split = GDN_STEP_CHUNKS // 2
    prepared = [prepare(t * c) for t in range(GDN_STEP_CHUNKS)]
    lows = lambda chunks: [low for p in chunks for low in p["low"]]
    t_early, t_late = [], []
    for _ in inverse(lows(prepared[:split]), t_early):
        pass
    solved = [solve(t * c, prepared[t], t_early[t * n_grp:(t + 1) * n_grp]) for t in range(split)]
    late = inverse(lows(prepared[split:]), t_late)
    for t in range(split):
        recur(t * c, prepared[t], solved[t], between=lambda: next(late, None))
    for _ in late:
        pass
    for t in range(split, GDN_STEP_CHUNKS):
        uw = solve(t * c, prepared[t], t_late[(t - split) * n_grp:(t - split + 1) * n_grp])
        recur(t * c, prepared[t], uw)


def _gdn_core(qkv, qkvz, gates, gates_t, out_gain, b, s):
    c = GDN_STEP_CHUNKS * GDN_CHUNK
    n = s // c
    m = b * s
    m_tril, m_strict, block_sel = _gdn_masks()
    row = lambda bi, ni: bi * n + ni
    const2 = lambda bi, ni: (0, 0)
    return pl.pallas_call(
        _gdn_kernel,
        grid=(b, n),
        in_specs=[pl.BlockSpec((c, GDN_CONV_DIM), lambda bi, ni: (row(bi, ni), 0)),
                  pl.BlockSpec((c, GDN_V), lambda bi, ni: (row(bi, ni), GDN_CONV_DIM // GDN_V)),
                  pl.BlockSpec((c, LANES), lambda bi, ni: (row(bi, ni), 0)),
                  pl.BlockSpec((LANES, c), lambda bi, ni: (0, row(bi, ni))),
                  pl.BlockSpec((1, GDN_HEAD_DIM), const2),
                  pl.BlockSpec((GDN_CHUNK, GDN_ROWS), const2),
                  pl.BlockSpec((GDN_CHUNK, GDN_ROWS), const2),
                  pl.BlockSpec((GDN_ROWS, GDN_ROWS), const2)],
        out_specs=pl.BlockSpec((c, GDN_V), lambda bi, ni: (row(bi, ni), 0)),
        out_shape=jax.ShapeDtypeStruct((m, GDN_V), BF16),
        scratch_shapes=[pltpu.VMEM((GDN_V_HEADS, GDN_HEAD_DIM, GDN_HEAD_DIM), F32)],
        compiler_params=_params("parallel", "arbitrary"),
        name="gdn_core",
    )(qkv, qkvz, gates, gates_t, out_gain, m_tril, m_strict, block_sel)


def kernel(x, positions, norm_mix, norm_mlp, w_up, w_down, ret_w_in, ret_w_out, swa_w_in, swa_q_gain,
           swa_k_gain, swa_sinks, swa_w_out, gdn_w_in, gdn_conv_w, gdn_a_log, gdn_dt_bias, gdn_out_gain,
           gdn_w_out):
    b, s, d = x.shape
    m = b * s
    depth = norm_mix.shape[0]
    x2d = x.reshape(m, d)
    pos_col = positions.reshape(m, 1).astype(F32)
    ret_cos, ret_sin = _rope_tables(pos_col, RET_DK, RET_DK)
    swa_cos, swa_sin = _rope_tables(pos_col, SWA_HEAD_DIM, LANES)

    for i in range(depth):
        mixer, j = i % 3, i // 3
        gain = norm_mix[i][None, :]
        gate = dict()
        if mixer == 0:
            proj = _inproj(x2d, gain, ret_w_in[j].astype(BF16), tn=1024)
            y = _retention_core(proj, ret_cos, ret_sin, b, s)
            w_out = ret_w_out[j]
            gate_cols = (2 * RET_QK + RET_V) // RET_V
            gate = dict(gate=_ret_gate, gate_pieces=RET_HEADS, gate_args=(proj,),
                        gate_specs=lambda tm, idx: (pl.BlockSpec((tm, RET_V), lambda r: (idx(r), gate_cols)),))
        elif mixer == 1:
            proj = _inproj(x2d, gain, swa_w_in[j].astype(BF16), tn=SWA_Q + 2 * SWA_KV)
            y = _swa_core(proj, swa_cos, swa_sin, swa_q_gain[j], swa_k_gain[j], swa_sinks[j], b, s)
            w_out = swa_w_out[j]
        else:
            n_main = GDN_CONV_DIM + GDN_V
            w_main = gdn_w_in[j][:, :n_main].astype(BF16)
            w_gate = jnp.pad(gdn_w_in[j][:, n_main:], ((0, 0), (0, LANES - 2 * GDN_V_HEADS))).astype(BF16)
            proj, ba = _inproj(x2d, gain, w_main, tn=1024, w_aux=w_gate)
            lane_pad = (GDN_V_HEADS, LANES - 2 * GDN_V_HEADS)
            alog_row = jnp.pad(gdn_a_log[j].astype(F32), lane_pad)[None, :]
            dtb_row = jnp.pad(gdn_dt_bias[j].astype(F32), lane_pad)[None, :]
            qkv_act, gates, gates_t = _gdn_conv(proj, ba, gdn_conv_w[j].astype(F32), alog_row, dtb_row, b, s)
            y = _gdn_core(qkv_act, proj, gates, gates_t, gdn_out_gain[j].astype(F32)[None, :], b, s)
            w_out = gdn_w_out[j]
        x2d = _outproj_mlp(x2d, y, w_out.astype(BF16), norm_mlp[i][None, :],
                           w_up[i].astype(BF16), w_down[i].astype(BF16), **gate)
    return x2d.reshape(b, s, d)
```
